```python
import jax, jax.numpy as jnp
from jax import lax
import numpy as np

D_MODEL = 1024
BATCH = 2
SEQ = 8192
DEPTH = 4
DEC_BATCH = 128
DEC_SEQ = 4
PAST_LEN = 8192
PAGE_SIZE = 128

MLA_HEADS = 8
MLA_NOPE = 64
MLA_ROPE = 32
MLA_V = 64
Q_LORA = 256
KV_LORA = 128
ROPE_THETA = 10000.0
MLA_SCALE = (MLA_NOPE + MLA_ROPE) ** -0.5

SB_HEADS = 8
SB_KV_HEADS = 4
SB_GROUP = SB_HEADS // SB_KV_HEADS
SB_HEAD_DIM = 64
SB_SCALE = SB_HEAD_DIM ** -0.5

D_FF = 2816
N_EXPERTS = 8
TOP_K = 2
D_FF_EXPERT = 3584
N_DENSE = (DEPTH + 1) // 2
N_MOE = DEPTH // 2

Q_BLOCK = 128
EPS = 1e-6
NEG_INF = -1e30

IN_SIZES = (Q_LORA, KV_LORA, MLA_ROPE, SB_HEADS * SB_HEAD_DIM, SB_KV_HEADS * SB_HEAD_DIM,
            SB_KV_HEADS * SB_HEAD_DIM, D_MODEL, D_MODEL)
IN_WIDTH = sum(IN_SIZES)
SPLIT_POINTS = tuple(int(v) for v in np.cumsum(IN_SIZES)[:-1])

kernel_name = 'hybrid_mla_stickbreak_gated_decoder'


def _rms(x, g):
    x32 = x.astype(jnp.float32)
    y = x32 * lax.rsqrt(jnp.mean(x32 * x32, axis=-1, keepdims=True) + EPS)
    return (y * g.astype(jnp.float32)).astype(x.dtype)


def _rope(x, pos):
    half = MLA_ROPE // 2
    inv_freq = ROPE_THETA ** (-jnp.arange(half, dtype=jnp.float32) / half)
    ang = pos.astype(jnp.float32)[:, None] * inv_freq[None, :]
    cos = jnp.cos(ang)[None, :, None, :].astype(x.dtype)
    sin = jnp.sin(ang)[None, :, None, :].astype(x.dtype)
    x1, x2 = x[..., :half], x[..., half:]
    return jnp.concatenate([x1 * cos - x2 * sin, x2 * cos + x1 * sin], axis=-1)


def _project(xn, pos, w_in, g_q_lat, w_q_up, g_q_nope, g_q_rope, g_kv_lat, g_k_rope):
    B, T, _ = xn.shape
    z = xn @ w_in
    q_lat, c_kv, k_rope, sb_q, sb_k, sb_v, gate_a, gate_b = jnp.split(z, SPLIT_POINTS, axis=-1)
    q = (_rms(q_lat, g_q_lat) @ w_q_up).reshape(B, T, MLA_HEADS, MLA_NOPE + MLA_ROPE)
    q_nope = _rms(q[..., :MLA_NOPE], g_q_nope)
    q_rope = _rope(_rms(q[..., MLA_NOPE:], g_q_rope), pos)
    c_kv = _rms(c_kv, g_kv_lat)
    k_rope = _rope(_rms(k_rope, g_k_rope)[:, :, None, :], pos)[:, :, 0, :]
    sb_q = sb_q.reshape(B, T, SB_HEADS, SB_HEAD_DIM)
    sb_k = sb_k.reshape(B, T, SB_KV_HEADS, SB_HEAD_DIM)
    sb_v = sb_v.reshape(B, T, SB_KV_HEADS, SB_HEAD_DIM)
    return q_nope, q_rope, c_kv, k_rope, sb_q, sb_k, sb_v, gate_a, gate_b


def _mla_key_nope(c, w_uk, g_k_nope):
    return _rms(jnp.einsum('bsc,chd->bshd', c, w_uk), g_k_nope)


def _mla_attend(q_nope, q_rope, q_pos, k_nope, k_rope, c, w_uv):
    S = c.shape[1]
    s = (jnp.einsum('bqhd,bshd->bhqs', q_nope, k_nope)
         + jnp.einsum('bqhr,bsr->bhqs', q_rope, k_rope)).astype(jnp.float32) * MLA_SCALE
    mask = jnp.arange(S, dtype=jnp.int32)[None, :] <= q_pos[:, None]
    p = jax.nn.softmax(jnp.where(mask, s, NEG_INF), axis=-1).astype(c.dtype)
    o_lat = jnp.einsum('bhqs,bsc->bqhc', p, c)
    return jnp.einsum('bqhc,chv->bqhv', o_lat, w_uv)


def _sb_attend(q, k, v, q_pos):
    B, Q, H, D = q.shape
    S = k.shape[1]
    qg = q.reshape(B, Q, SB_KV_HEADS, SB_GROUP, D)
    z = jnp.einsum('bqkgd,bskd->bkgqs', qg, k).astype(jnp.float32) * SB_SCALE
    mask = jnp.arange(S, dtype=jnp.int32)[None, :] < q_pos[:, None]
    log_beta = jax.nn.log_sigmoid(z)
    log_1m = jnp.where(mask, jax.nn.log_sigmoid(-z), 0.0)
    after = lax.cumsum(log_1m, axis=z.ndim - 1, reverse=True) - log_1m
    w = jnp.where(mask, jnp.exp(log_beta + after), 0.0).astype(v.dtype)
    o = jnp.einsum('bkgqs,bskd->bqkgd', w, v)
    return o.reshape(B, Q, H, D)


def _prompt_attend(qn, qr, c, kr, sq, sk, sv, w_uk, g_k_nope, w_uv):
    B, T = qn.shape[:2]
    nb = T // Q_BLOCK
    k_nope = _mla_key_nope(c, w_uk, g_k_nope)

    def to_blocks(a):
        return a.reshape((B, nb, Q_BLOCK) + a.shape[2:]).swapaxes(0, 1)

    def from_blocks(a):
        return a.swapaxes(0, 1).reshape((B, T) + a.shape[3:])

    starts = jnp.arange(nb, dtype=jnp.int32) * Q_BLOCK

    def block(args):
        qn_b, qr_b, sq_b, t0 = args
        q_pos = t0 + jnp.arange(Q_BLOCK, dtype=jnp.int32)
        return (_mla_attend(qn_b, qr_b, q_pos, k_nope, kr, c, w_uv),
                _sb_attend(sq_b, sk, sv, q_pos))

    o_a, o_b = lax.map(block, (to_blocks(qn), to_blocks(qr), to_blocks(sq), starts))
    return from_blocks(o_a), from_blocks(o_b)


def _sample_attend(qn, qr, c, kr, sq, sk, sv, page_table, cache_lat, cache_kr, cache_k, cache_v,
                   layer, w_uk, g_k_nope, w_uv):
    dec_seq = qn.shape[1]
    q_pos = PAST_LEN + jnp.arange(dec_seq, dtype=jnp.int32)

    def past(cache, pages):
        g = cache[layer, pages]
        return g.reshape((-1,) + g.shape[2:])

    def one(args):
        qn_i, qr_i, sq_i, c_i, kr_i, sk_i, sv_i, pages = args
        c_all = jnp.concatenate([past(cache_lat, pages), c_i], axis=0)[None]
        kr_all = jnp.concatenate([past(cache_kr, pages), kr_i], axis=0)[None]
        k_all = jnp.concatenate([past(cache_k, pages), sk_i], axis=0)[None]
        v_all = jnp.concatenate([past(cache_v, pages), sv_i], axis=0)[None]
        k_nope = _mla_key_nope(c_all, w_uk, g_k_nope)
        o_a = _mla_attend(qn_i[None], qr_i[None], q_pos, k_nope, kr_all, c_all, w_uv)[0]
        o_b = _sb_attend(sq_i[None], k_all, v_all, q_pos)[0]
        return o_a, o_b

    return lax.map(one, (qn, qr, sq, c, kr, sk, sv, page_table))


def _merge(o_a, o_b, gate_a, gate_b, w_mla_out, w_sb_out, w_o):
    B, T = o_a.shape[:2]
    br_a = o_a.reshape(B, T, -1) @ w_mla_out
    br_b = o_b.reshape(B, T, -1) @ w_sb_out
    return (jax.nn.sigmoid(gate_a) * br_a + jax.nn.sigmoid(gate_b) * br_b) @ w_o


def _swiglu(x, w_g, w_u, w_d):
    return (jax.nn.silu(x @ w_g) * (x @ w_u)) @ w_d


def _moe(x, w_r, w_g, w_u, w_d):
    logits = (x @ w_r).astype(jnp.float32)
    top_v, top_i = lax.top_k(logits, TOP_K)
    top_w = jax.nn.softmax(top_v, axis=-1)
    gates = jnp.sum(jax.nn.one_hot(top_i, N_EXPERTS, dtype=jnp.float32) * top_w[..., None], axis=-2).astype(x.dtype)
    out = jnp.zeros_like(x)
    for e in range(N_EXPERTS):
        out = out + gates[..., e:e + 1] * _swiglu(x, w_g[e], w_u[e], w_d[e])
    return out


def setup_inputs(seed: int = 0) -> dict:
    key = jax.random.key(seed)
    k = jax.random.split(key, 32)
    f32 = jnp.float32

    def nrm(i, shape, scale=1.0):
        return jax.random.normal(k[i], shape, f32) * scale

    def gain(i, shape):
        return 1.0 + 0.1 * jax.random.normal(k[i], shape, f32)

    n_pages = PAST_LEN // PAGE_SIZE
    n_pool = (DEC_BATCH * n_pages * 5) // 4
    page_table = jax.random.permutation(k[6], n_pool)[: DEC_BATCH * n_pages].reshape(DEC_BATCH, n_pages).astype(jnp.int32)
    return {
        'x_prompt': nrm(0, (BATCH, SEQ, D_MODEL)),
        'x_sample': nrm(1, (DEC_BATCH, DEC_SEQ, D_MODEL)),
        'cache_mla_latent': nrm(2, (DEPTH, n_pool, PAGE_SIZE, KV_LORA)),
        'cache_mla_krope': nrm(3, (DEPTH, n_pool, PAGE_SIZE, MLA_ROPE)),
        'cache_sb_k': nrm(4, (DEPTH, n_pool, PAGE_SIZE, SB_KV_HEADS, SB_HEAD_DIM)),
        'cache_sb_v': nrm(5, (DEPTH, n_pool, PAGE_SIZE, SB_KV_HEADS, SB_HEAD_DIM)),
        'page_table': page_table,
        'g_attn_norm': gain(7, (DEPTH, D_MODEL)),
        'w_in': nrm(8, (DEPTH, D_MODEL, IN_WIDTH), D_MODEL ** -0.5),
        'g_q_lat': gain(9, (DEPTH, Q_LORA)),
        'w_q_up': nrm(10, (DEPTH, Q_LORA, MLA_HEADS * (MLA_NOPE + MLA_ROPE)), Q_LORA ** -0.5),
        'g_q_nope': gain(11, (DEPTH, MLA_NOPE)),
        'g_q_rope': gain(12, (DEPTH, MLA_ROPE)),
        'g_kv_lat': gain(13, (DEPTH, KV_LORA)),
        'g_k_rope': gain(14, (DEPTH, MLA_ROPE)),
        'w_uk': nrm(15, (DEPTH, KV_LORA, MLA_HEADS, MLA_NOPE), KV_LORA ** -0.5),
        'g_k_nope': gain(16, (DEPTH, MLA_NOPE)),
        'w_uv': nrm(17, (DEPTH, KV_LORA, MLA_HEADS, MLA_V), KV_LORA ** -0.5),
        'w_mla_out': nrm(18, (DEPTH, MLA_HEADS * MLA_V, D_MODEL), (MLA_HEADS * MLA_V) ** -0.5),
        'w_sb_out': nrm(19, (DEPTH, SB_HEADS * SB_HEAD_DIM, D_MODEL), (SB_HEADS * SB_HEAD_DIM) ** -0.5),
        'w_o': nrm(20, (DEPTH, D_MODEL, D_MODEL), D_MODEL ** -0.5),
        'g_ffn_norm': gain(21, (DEPTH, D_MODEL)),
        'w_ffn_gate': nrm(22, (N_DENSE, D_MODEL, D_FF), D_MODEL ** -0.5),
        'w_ffn_up': nrm(23, (N_DENSE, D_MODEL, D_FF), D_MODEL ** -0.5),
        'w_ffn_down': nrm(24, (N_DENSE, D_FF, D_MODEL), D_FF ** -0.5),
        'w_router': nrm(25, (N_MOE, D_MODEL, N_EXPERTS), D_MODEL ** -0.5),
        'w_exp_gate': nrm(26, (N_MOE, N_EXPERTS, D_MODEL, D_FF_EXPERT), D_MODEL ** -0.5),
        'w_exp_up': nrm(27, (N_MOE, N_EXPERTS, D_MODEL, D_FF_EXPERT), D_MODEL ** -0.5),
        'w_exp_down': nrm(28, (N_MOE, N_EXPERTS, D_FF_EXPERT, D_MODEL), D_FF_EXPERT ** -0.5),
    }


def reference(x_prompt, x_sample, cache_mla_latent, cache_mla_krope, cache_sb_k, cache_sb_v, page_table,
              g_attn_norm, w_in, g_q_lat, w_q_up, g_q_nope, g_q_rope, g_kv_lat, g_k_rope, w_uk, g_k_nope,
              w_uv, w_mla_out, w_sb_out, w_o, g_ffn_norm, w_ffn_gate, w_ffn_up, w_ffn_down, w_router,
              w_exp_gate, w_exp_up, w_exp_down):
    pos_p = jnp.arange(x_prompt.shape[1], dtype=jnp.int32)
    pos_s = PAST_LEN + jnp.arange(x_sample.shape[1], dtype=jnp.int32)
    hp, hs = x_prompt, x_sample
    p_lat, p_kr, p_k, p_v = [], [], [], []
    s_lat, s_kr, s_k, s_v = [], [], [], []
    for l in range(DEPTH):
        proj = (w_in[l], g_q_lat[l], w_q_up[l], g_q_nope[l], g_q_rope[l], g_kv_lat[l], g_k_rope[l])
        out_w = (w_mla_out[l], w_sb_out[l], w_o[l])

        qn, qr, c, kr, sq, sk, sv, ga, gb = _project(_rms(hp, g_attn_norm[l]), pos_p, *proj)
        o_a, o_b = _prompt_attend(qn, qr, c, kr, sq, sk, sv, w_uk[l], g_k_nope[l], w_uv[l])
        hp = hp + _merge(o_a, o_b, ga, gb, *out_w)
        p_lat.append(c); p_kr.append(kr); p_k.append(sk); p_v.append(sv)

        qn, qr, c, kr, sq, sk, sv, ga, gb = _project(_rms(hs, g_attn_norm[l]), pos_s, *proj)
        o_a, o_b = _sample_attend(qn, qr, c, kr, sq, sk, sv, page_table, cache_mla_latent, cache_mla_krope,
                                  cache_sb_k, cache_sb_v, l, w_uk[l], g_k_nope[l], w_uv[l])
        hs = hs + _merge(o_a, o_b, ga, gb, *out_w)
        s_lat.append(c); s_kr.append(kr); s_k.append(sk); s_v.append(sv)

        j = l // 2
        np_ = _rms(hp, g_ffn_norm[l])
        ns_ = _rms(hs, g_ffn_norm[l])
        if l % 2 == 0:
            hp = hp + _swiglu(np_, w_ffn_gate[j], w_ffn_up[j], w_ffn_down[j])
            hs = hs + _swiglu(ns_, w_ffn_gate[j], w_ffn_up[j], w_ffn_down[j])
        else:
            hp = hp + _moe(np_, w_router[j], w_exp_gate[j], w_exp_up[j], w_exp_down[j])
            hs = hs + _moe(ns_, w_router[j], w_exp_gate[j], w_exp_up[j], w_exp_down[j])

    return (hp, hs, jnp.stack(p_lat), jnp.stack(p_kr), jnp.stack(p_k), jnp.stack(p_v),
            jnp.stack(s_lat), jnp.stack(s_kr), jnp.stack(s_k), jnp.stack(s_v))
```

```python
import functools
import math

import jax
import jax.numpy as jnp
import numpy as np
from jax import lax
from jax.experimental import pallas as pl
from jax.experimental.pallas import tpu as pltpu

D_MODEL = 1024
MLA_HEADS = 8
MLA_NOPE = 64
MLA_ROPE = 32
MLA_V = 64
Q_LORA = 256
KV_LORA = 128
ROPE_THETA = 10000.0
MLA_SCALE = (MLA_NOPE + MLA_ROPE) ** -0.5
SB_HEADS = 8
SB_KV_HEADS = 4
SB_GROUP = SB_HEADS // SB_KV_HEADS
SB_HEAD_DIM = 64
SB_SCALE = SB_HEAD_DIM ** -0.5
N_EXPERTS = 8
EPS = 1e-6
NEG_INF = -1e30

LANES = 128
HALF_ROPE = MLA_ROPE // 2
NOPE_LO, NOPE_HI = MLA_ROPE, MLA_ROPE + MLA_NOPE
VMEM_LIMIT = 56 * 1024 * 1024

F32 = jnp.float32
BF16 = jnp.bfloat16


def _pick(n, candidates):
    for c in candidates:
        if n % c == 0:
            return c
    return n


def _mm(a, b):
    return jnp.dot(a, b, preferred_element_type=F32)


def _mm_nt(a, b):
    return lax.dot_general(a, b, (((1,), (1,)), ((), ())), preferred_element_type=F32)


def _cparams(sem):
    return pltpu.CompilerParams(dimension_semantics=sem, vmem_limit_bytes=VMEM_LIMIT)


_C_QLAT, _C_CKV, _C_KR, _C_SQ, _C_SK, _C_SV, _C_G, _C_END = 0, 256, 384, 512, 1024, 1280, 1536, 3584


def _rope_rotate(y, cos, sina, sinb):
    return y * cos + pltpu.roll(y, LANES - HALF_ROPE, 1) * sina + pltpu.roll(y, HALF_ROPE, 1) * sinb


def _proj_kernel(h_ref, tab_ref, gv_ref, w_ref, wqu_ref, wuk_ref,
                 qcat_ref, kcat_ref, c_ref, cbf_ref, kr_ref, sq_ref, sk_ref, sv_ref,
                 skbf_ref, svbf_ref, gates_ref):
    x = h_ref[...]
    g_attn = gv_ref[0:1, :]
    xn = (x * lax.rsqrt(jnp.mean(x * x, axis=-1, keepdims=True) + EPS) * g_attn).astype(BF16)

    cos = tab_ref[:, 0:LANES]
    sina = tab_ref[:, LANES:2 * LANES]
    sinb = tab_ref[:, 2 * LANES:3 * LANES]
    lane = lax.broadcasted_iota(jnp.int32, (1, LANES), 1)
    m_rope = (lane < MLA_ROPE).astype(F32)
    m_nope = ((lane >= NOPE_LO) & (lane < NOPE_HI)).astype(F32)

    q_lat = _mm(xn, w_ref[:, _C_QLAT:_C_CKV])
    g_ql = gv_ref[1:2, 0:Q_LORA]
    ql = (q_lat * lax.rsqrt(jnp.mean(q_lat * q_lat, axis=-1, keepdims=True) + EPS) * g_ql).astype(BF16)
    q = _mm(ql, wqu_ref[...])
    g_qr = gv_ref[2:3, 0:LANES]
    g_qn = gv_ref[3:4, 0:LANES]
    for h in range(MLA_HEADS):
        xh = q[:, h * LANES:(h + 1) * LANES]
        x2 = xh * xh
        ss_r = jnp.sum(x2 * m_rope, axis=-1, keepdims=True)
        ss_n = jnp.sum(x2 * m_nope, axis=-1, keepdims=True)
        sc = lax.rsqrt(ss_r * (1.0 / MLA_ROPE) + EPS) * g_qr + lax.rsqrt(ss_n * (1.0 / MLA_NOPE) + EPS) * g_qn
        y = _rope_rotate(xh * sc, cos, sina, sinb)
        qcat_ref[:, h * LANES:(h + 1) * LANES] = y.astype(BF16)

    c_raw = _mm(xn, w_ref[:, _C_CKV:_C_KR])
    g_kvl = gv_ref[4:5, 0:KV_LORA]
    c = c_raw * lax.rsqrt(jnp.mean(c_raw * c_raw, axis=-1, keepdims=True) + EPS) * g_kvl
    c_ref[...] = c
    cbf = c.astype(BF16)
    cbf_ref[...] = cbf

    kr_raw = _mm(xn, w_ref[:, _C_KR:_C_SQ])
    g_kr = gv_ref[5:6, 0:LANES]
    ss = jnp.sum(kr_raw * kr_raw, axis=-1, keepdims=True)
    krn = _rope_rotate(kr_raw * (lax.rsqrt(ss * (1.0 / MLA_ROPE) + EPS) * g_kr), cos, sina, sinb)
    kr_ref[...] = krn[:, 0:MLA_ROPE]

    kp = _mm(cbf, wuk_ref[...])
    g_kn = gv_ref[6:7, 0:LANES]
    for h in range(MLA_HEADS):
        kh = kp[:, h * LANES:(h + 1) * LANES]
        ss_n = jnp.sum(kh * kh, axis=-1, keepdims=True)
        kn = kh * (lax.rsqrt(ss_n * (1.0 / MLA_NOPE) + EPS) * g_kn)
        kcat_ref[:, h * LANES:(h + 1) * LANES] = (kn + krn).astype(BF16)

    sq_ref[...] = _mm(xn, w_ref[:, _C_SQ:_C_SK]).astype(BF16)
    sk = _mm(xn, w_ref[:, _C_SK:_C_SV])
    sk_ref[...] = sk
    skbf_ref[...] = sk.astype(BF16)
    sv = _mm(xn, w_ref[:, _C_SV:_C_G])
    sv_ref[...] = sv
    svbf_ref[...] = sv.astype(BF16)

    g = _mm(xn, w_ref[:, _C_G:_C_END])
    gates_ref[...] = (1.0 / (1.0 + jnp.exp(-g))).astype(BF16)


def _proj_call(h, tab, gv, w_all, wqu, wuk, tm):
    T = h.shape[0]
    row = lambda w: pl.BlockSpec((tm, w), lambda i: (i, 0))
    full = lambda a: pl.BlockSpec(a.shape, lambda i: (0, 0))
    out_shapes = (
        jax.ShapeDtypeStruct((T, MLA_HEADS * LANES), BF16),
        jax.ShapeDtypeStruct((T, MLA_HEADS * LANES), BF16),
        jax.ShapeDtypeStruct((T, KV_LORA), F32),
        jax.ShapeDtypeStruct((T, KV_LORA), BF16),
        jax.ShapeDtypeStruct((T, MLA_ROPE), F32),
        jax.ShapeDtypeStruct((T, SB_HEADS * SB_HEAD_DIM), BF16),
        jax.ShapeDtypeStruct((T, SB_KV_HEADS * SB_HEAD_DIM), F32),
        jax.ShapeDtypeStruct((T, SB_KV_HEADS * SB_HEAD_DIM), F32),
        jax.ShapeDtypeStruct((T, SB_KV_HEADS * SB_HEAD_DIM), BF16),
        jax.ShapeDtypeStruct((T, SB_KV_HEADS * SB_HEAD_DIM), BF16),
        jax.ShapeDtypeStruct((T, 2 * D_MODEL), BF16),
    )
    return pl.pallas_call(
        _proj_kernel,
        grid=(T // tm,),
        in_specs=[row(D_MODEL), row(3 * LANES), full(gv), full(w_all), full(wqu), full(wuk)],
        out_specs=tuple(row(s.shape[1]) for s in out_shapes),
        out_shape=out_shapes,
        compiler_params=_cparams(("parallel",)),
        name="proj",
    )(h, tab, gv, w_all, wqu, wuk)


def _mla_kernel(qi_tab, ki_tab, q_ref, k_ref, v_ref, wuv_ref, o_ref, m_scr, l_scr, acc_scr, *, tq):
    t = pl.program_id(1)
    qi = qi_tab[t]
    ki = ki_tab[t]

    @pl.when(ki == 0)
    def _():
        m_scr[...] = jnp.full(m_scr.shape, NEG_INF, F32)
        l_scr[...] = jnp.zeros(l_scr.shape, F32)
        acc_scr[...] = jnp.zeros(acc_scr.shape, F32)

    def step(masked):
        v = v_ref[...]
        if masked:
            row = lax.broadcasted_iota(jnp.int32, (tq, tq), 0)
            col = lax.broadcasted_iota(jnp.int32, (tq, tq), 1)
            keep = col <= row
        for h in range(MLA_HEADS):
            s = _mm_nt(q_ref[:, h * LANES:(h + 1) * LANES], k_ref[:, h * LANES:(h + 1) * LANES])
            if masked:
                s = jnp.where(keep, s, NEG_INF)
            m_prev = m_scr[h]
            m_new = jnp.maximum(m_prev, jnp.max(s, axis=1, keepdims=True))
            alpha = jnp.exp(m_prev - m_new)
            p = jnp.exp(s - m_new[:, 0:1])
            l_scr[h] = alpha * l_scr[h] + jnp.sum(p, axis=1, keepdims=True)
            acc_scr[h] = alpha * acc_scr[h] + _mm(p.astype(BF16), v)
            m_scr[h] = m_new

    @pl.when(ki < qi)
    def _():
        step(False)

    @pl.when(ki == qi)
    def _():
        step(True)
        o_lat = jnp.concatenate([(acc_scr[h] / l_scr[h]).astype(BF16) for h in range(MLA_HEADS)], axis=1)
        o_ref[...] = _mm(o_lat, wuv_ref[...]).astype(BF16)


def _tri_tables(nq, descending):
    qi, ki = [], []
    for i in range(nq):
        ks = range(i, -1, -1) if descending else range(i + 1)
        for j in ks:
            qi.append(i)
            ki.append(j)
    return jnp.asarray(np.array(qi, np.int32)), jnp.asarray(np.array(ki, np.int32))


def _mla_call(qcat, kcat, cbf, wuv_bd, batch, seq, tq):
    nq = seq // tq
    qi_tab, ki_tab = _tri_tables(nq, descending=False)
    n_tri = int(qi_tab.shape[0])
    return pl.pallas_call(
        functools.partial(_mla_kernel, tq=tq),
        grid_spec=pltpu.PrefetchScalarGridSpec(
            num_scalar_prefetch=2,
            grid=(batch, n_tri),
            in_specs=[
                pl.BlockSpec((tq, MLA_HEADS * LANES), lambda b, t, qt, kt: (b * nq + qt[t], 0)),
                pl.BlockSpec((tq, MLA_HEADS * LANES), lambda b, t, qt, kt: (b * nq + kt[t], 0)),
                pl.BlockSpec((tq, KV_LORA), lambda b, t, qt, kt: (b * nq + kt[t], 0)),
                pl.BlockSpec(wuv_bd.shape, lambda b, t, qt, kt: (0, 0)),
            ],
            out_specs=pl.BlockSpec((tq, MLA_HEADS * MLA_V), lambda b, t, qt, kt: (b * nq + qt[t], 0)),
            scratch_shapes=[
                pltpu.VMEM((MLA_HEADS, tq, LANES), F32),
                pltpu.VMEM((MLA_HEADS, tq, LANES), F32),
                pltpu.VMEM((MLA_HEADS, tq, KV_LORA), F32),
            ],
        ),
        out_shape=jax.ShapeDtypeStruct((batch * seq, MLA_HEADS * MLA_V), BF16),
        compiler_params=_cparams(("parallel", "arbitrary")),
        name="mla_prompt",
    )(qi_tab, ki_tab, qcat, kcat, cbf, wuv_bd)


SB_CHUNK = 256


def _suffix_matrix(n):
    j = np.arange(n)[:, None]
    s = np.arange(n)[None, :]
    t = (j > s).astype(np.float32)
    return jnp.asarray(np.concatenate([t, t], axis=0), dtype=BF16)


def _sb_terms(z):
    lp = jnp.log(1.0 + jnp.exp(-jnp.abs(z)))
    log_beta = jnp.minimum(z, 0.0) - lp
    return log_beta, log_beta - z


def _suffix_sum(l1, tt):
    hi = l1.astype(BF16)
    lo = (l1 - hi.astype(F32)).astype(BF16)
    return _mm(jnp.concatenate([hi, lo], axis=1), tt)


def _sb_kernel(qi_tab, ki_tab, q_ref, k_ref, v_ref, tt_ref, o_ref, carry_scr, acc_scr, *, tq, ck):
    t = pl.program_id(2)
    qi = qi_tab[t]
    ki = ki_tab[t]
    n_chunks = tq // ck

    @pl.when(ki == qi)
    def _():
        carry_scr[...] = jnp.zeros(carry_scr.shape, F32)
        acc_scr[...] = jnp.zeros(acc_scr.shape, F32)

    def step(masked):
        tt = tt_ref[...]
        lane = lax.broadcasted_iota(jnp.int32, (1, LANES), 1)
        first_head = lane < SB_HEAD_DIM
        for c in range(n_chunks - 1, -1, -1):
            k = k_ref[c * ck:(c + 1) * ck, :]
            v = v_ref[c * ck:(c + 1) * ck, :]
            if masked:
                row = lax.broadcasted_iota(jnp.int32, (tq, ck), 0)
                col = lax.broadcasted_iota(jnp.int32, (tq, ck), 1) + c * ck
                keep = col < row
            outs = []
            for g in range(SB_GROUP):
                z = _mm_nt(q_ref[:, g * SB_HEAD_DIM:(g + 1) * SB_HEAD_DIM], k)
                log_beta, l1 = _sb_terms(z)
                if masked:
                    l1 = jnp.where(keep, l1, 0.0)
                after = _suffix_sum(l1, tt)
                carry = carry_scr[g]
                w = jnp.exp(log_beta + after + carry[:, 0:1])
                if masked:
                    w = jnp.where(keep, w, 0.0)
                outs.append(_mm(w.astype(BF16), v))
                carry_scr[g] = carry + jnp.sum(l1, axis=1, keepdims=True)
            acc_scr[...] += jnp.where(first_head, outs[0], outs[1])

    @pl.when(ki < qi)
    def _():
        step(False)

    @pl.when(ki == qi)
    def _():
        step(True)

    @pl.when(ki == 0)
    def _():
        o_ref[...] = acc_scr[...].astype(BF16)


def _sb_call(sq, k_heads, v_heads, tt, batch, seq, tq, ck):
    nq = seq // tq
    qi_tab, ki_tab = _tri_tables(nq, descending=True)
    n_tri = int(qi_tab.shape[0])
    gw = SB_GROUP * SB_HEAD_DIM
    return pl.pallas_call(
        functools.partial(_sb_kernel, tq=tq, ck=ck),
        grid_spec=pltpu.PrefetchScalarGridSpec(
            num_scalar_prefetch=2,
            grid=(batch, SB_KV_HEADS, n_tri),
            in_specs=[
                pl.BlockSpec((tq, gw), lambda b, kv, t, qt, kt: (b * nq + qt[t], kv)),
                pl.BlockSpec((None, None, tq, SB_HEAD_DIM), lambda b, kv, t, qt, kt: (b, kv, kt[t], 0)),
                pl.BlockSpec((None, None, tq, gw), lambda b, kv, t, qt, kt: (b, kv, kt[t], 0)),
                pl.BlockSpec(tt.shape, lambda b, kv, t, qt, kt: (0, 0)),
            ],
            out_specs=pl.BlockSpec((tq, gw), lambda b, kv, t, qt, kt: (b * nq + qt[t], kv)),
            scratch_shapes=[
                pltpu.VMEM((SB_GROUP, tq, LANES), F32),
                pltpu.VMEM((tq, gw), F32),
            ],
        ),
        out_shape=jax.ShapeDtypeStruct((batch * seq, SB_HEADS * SB_HEAD_DIM), BF16),
        compiler_params=_cparams(("parallel", "parallel", "arbitrary")),
        name="sb_prompt",
    )(qi_tab, ki_tab, sq, k_heads, v_heads, tt)


def _absorb_kernel(q_ref, w_ref, o_ref):
    for h in range(MLA_HEADS):
        o_ref[:, h * LANES:(h + 1) * LANES] = _mm(q_ref[:, h * LANES:(h + 1) * LANES], w_ref[h]).astype(BF16)


def _absorb_call(q_s, wabs):
    n = q_s.shape[0]
    return pl.pallas_call(
        _absorb_kernel,
        grid=(1,),
        in_specs=[pl.BlockSpec(q_s.shape, lambda i: (0, 0)), pl.BlockSpec(wabs.shape, lambda i: (0, 0, 0))],
        out_specs=pl.BlockSpec((n, MLA_HEADS * LANES), lambda i: (0, 0)),
        out_shape=jax.ShapeDtypeStruct((n, MLA_HEADS * LANES), BF16),
        compiler_params=_cparams(("arbitrary",)),
        name="absorb_q",
    )(q_s, wabs)


def _sample_kernel(*refs, n_pg, dec_seq, page):
    pt_ref = refs[0]
    del pt_ref
    qabs_ref, qr_ref, qbd_ref, wukt_ref, tt_ref, nlat_ref, nkr_ref, nk_ref, nv_ref = refs[1:10]
    lat_refs = refs[10:10 + n_pg]
    kr_refs = refs[10 + n_pg:10 + 2 * n_pg]
    k_refs = refs[10 + 2 * n_pg:10 + 3 * n_pg]
    v_refs = refs[10 + 3 * n_pg:10 + 4 * n_pg]
    olat_ref, osb_ref = refs[10 + 4 * n_pg:12 + 4 * n_pg]
    wst_scr, m_scr, l_scr, acc_scr, carry_scr, accsb_scr = refs[12 + 4 * n_pg:]

    j = pl.program_id(1)
    n_rows = MLA_HEADS * dec_seq
    n_sb = SB_HEADS * dec_seq
    kd = SB_KV_HEADS * SB_HEAD_DIM

    def mla_tile(c_pages, krt_pages, mask):
        cb = jnp.concatenate([p.astype(BF16) for p in c_pages], axis=0)
        krt = jnp.concatenate([p.astype(BF16) for p in krt_pages], axis=1)
        a = _mm_nt(wst_scr[...], cb)
        n_keys = a.shape[1]
        ssq = jnp.concatenate(
            [jnp.sum(jnp.square(a[h * MLA_NOPE:(h + 1) * MLA_NOPE]), axis=0, keepdims=True)
             for h in range(MLA_HEADS)], axis=0)
        r = lax.rsqrt(ssq * (1.0 / MLA_NOPE) + EPS)
        sn = a[MLA_HEADS * MLA_NOPE:]
        s = sn * jnp.concatenate([r] * dec_seq, axis=0) + _mm(qr_ref[...], krt)
        if mask is not None:
            s = jnp.where(mask(n_rows, n_keys, MLA_HEADS, True), s, NEG_INF)
        m_prev = m_scr[...]
        m_new = jnp.maximum(m_prev, jnp.max(s, axis=1, keepdims=True))
        alpha = jnp.exp(m_prev - m_new)
        p = jnp.exp(s - m_new[:, 0:1])
        l_scr[...] = alpha * l_scr[...] + jnp.sum(p, axis=1, keepdims=True)
        acc_scr[...] = alpha * acc_scr[...] + _mm(p.astype(BF16), cb)
        m_scr[...] = m_new

    def sb_tile(kt_pages, vt_pages, mask):
        tt = tt_ref[...]
        qbd = qbd_ref[...]
        n = len(kt_pages)
        z = jnp.concatenate([_mm(qbd, p.astype(BF16)) for p in kt_pages], axis=0)
        log_beta, l1 = _sb_terms(z)
        if mask is not None:
            keep = jnp.concatenate([mask(n_sb, page, dec_seq, False)] * n, axis=0)
            l1 = jnp.where(keep, l1, 0.0)
        after = _suffix_sum(l1, tt)
        tot = jnp.sum(l1, axis=1, keepdims=True)
        run = carry_scr[...]
        carries = [None] * n
        for i in range(n - 1, -1, -1):
            carries[i] = run
            run = run + tot[i * n_sb:(i + 1) * n_sb]
        carry_scr[...] = run
        carry = jnp.concatenate([c[:, 0:1] for c in carries], axis=0)
        w = jnp.exp(log_beta + after + carry)
        if mask is not None:
            w = jnp.where(keep, w, 0.0)
        wl = jnp.concatenate([w[i * n_sb:(i + 1) * n_sb].astype(BF16) for i in range(n)], axis=1)
        vt = jnp.concatenate([p.astype(BF16) for p in vt_pages], axis=1)
        accsb_scr[...] += _mm_nt(vt, wl)

    def new_mask(n_r, n_k, div, inclusive):
        row = lax.broadcasted_iota(jnp.int32, (n_r, n_k), 0)
        col = lax.broadcasted_iota(jnp.int32, (n_r, n_k), 1)
        qpos = row // div if inclusive else row % div
        return (col <= qpos) if inclusive else (col < qpos)

    @pl.when(j == 0)
    def _():
        wst_scr[0:MLA_HEADS * MLA_NOPE, :] = wukt_ref[...]
        wst_scr[MLA_HEADS * MLA_NOPE:, :] = qabs_ref[...]
        m_scr[...] = jnp.full(m_scr.shape, NEG_INF, F32)
        l_scr[...] = jnp.zeros(l_scr.shape, F32)
        acc_scr[...] = jnp.zeros(acc_scr.shape, F32)
        carry_scr[...] = jnp.zeros(carry_scr.shape, F32)
        accsb_scr[...] = jnp.zeros(accsb_scr.shape, F32)
        mla_tile([nlat_ref[...]], [nkr_ref[...]], new_mask)
        sb_tile([nk_ref[...]], [nv_ref[...]], new_mask)

    mla_tile([r[...] for r in lat_refs], [r[...] for r in kr_refs], None)
    sb_tile([r[...].reshape(kd, page) for r in k_refs], [r[...].reshape(kd, page) for r in v_refs], None)

    @pl.when(j == pl.num_programs(1) - 1)
    def _():
        olat_ref[...] = acc_scr[...] / l_scr[...]
        osb_ref[...] = accsb_scr[...]


def _sample_call(layer, page_table, qabs, qr, qbd, wukt, tt, nlat, nkr, nk, nv,
                 cache_lat, cache_krt, cache_kt, cache_vt, n_pg):
    dec_batch, n_pages = page_table.shape
    page = cache_lat.shape[2]
    dec_seq = qabs.shape[1] // MLA_HEADS
    n_steps = n_pages // n_pg
    n_rows = MLA_HEADS * dec_seq
    n_sb = SB_HEADS * dec_seq
    kd = SB_KV_HEADS * SB_HEAD_DIM

    def per_seq(a):
        return pl.BlockSpec((None,) + a.shape[1:], lambda b, j, pt: (b,) + (0,) * (a.ndim - 1))

    def const(a):
        return pl.BlockSpec(a.shape, lambda b, j, pt: (0,) * a.ndim)

    def paged(a, g):
        nd = a.ndim - 2
        return pl.BlockSpec(
            (None, None) + a.shape[2:],
            lambda b, j, pt: (layer, pt[b, (n_steps - 1 - j) * n_pg + g]) + (0,) * nd)

    in_specs = [per_seq(qabs), per_seq(qr), per_seq(qbd), const(wukt), const(tt),
                per_seq(nlat), per_seq(nkr), per_seq(nk), per_seq(nv)]
    operands = [qabs, qr, qbd, wukt, tt, nlat, nkr, nk, nv]
    for cache in (cache_lat, cache_krt, cache_kt, cache_vt):
        for g in range(n_pg):
            in_specs.append(paged(cache, g))
            operands.append(cache)

    return pl.pallas_call(
        functools.partial(_sample_kernel, n_pg=n_pg, dec_seq=dec_seq, page=page),
        grid_spec=pltpu.PrefetchScalarGridSpec(
            num_scalar_prefetch=1,
            grid=(dec_batch, n_steps),
            in_specs=in_specs,
            out_specs=(
                pl.BlockSpec((None, n_rows, KV_LORA), lambda b, j, pt: (b, 0, 0)),
                pl.BlockSpec((None, kd, n_sb), lambda b, j, pt: (b, 0, 0)),
            ),
            scratch_shapes=[
                pltpu.VMEM((MLA_HEADS * MLA_NOPE + n_rows, KV_LORA), BF16),
                pltpu.VMEM((n_rows, LANES), F32),
                pltpu.VMEM((n_rows, LANES), F32),
                pltpu.VMEM((n_rows, KV_LORA), F32),
                pltpu.VMEM((n_sb, LANES), F32),
                pltpu.VMEM((kd, n_sb), F32),
            ],
        ),
        out_shape=(
            jax.ShapeDtypeStruct((dec_batch, n_rows, KV_LORA), F32),
            jax.ShapeDtypeStruct((dec_batch, kd, n_sb), F32),
        ),
        compiler_params=_cparams(("parallel", "arbitrary")),
        name="sample_attn",
    )(page_table, *operands)


def _uv_kernel(x_ref, w_ref, o_ref):
    o_ref[...] = _mm(x_ref[...].astype(BF16), w_ref[...]).astype(BF16)


def _uv_call(o_lat, wuv_bd):
    n = o_lat.shape[0]
    return pl.pallas_call(
        _uv_kernel,
        grid=(1,),
        in_specs=[pl.BlockSpec(o_lat.shape, lambda i: (0, 0)), pl.BlockSpec(wuv_bd.shape, lambda i: (0, 0))],
        out_specs=pl.BlockSpec((n, MLA_HEADS * MLA_V), lambda i: (0, 0)),
        out_shape=jax.ShapeDtypeStruct((n, MLA_HEADS * MLA_V), BF16),
        compiler_params=_cparams(("arbitrary",)),
        name="sample_uv",
    )(o_lat, wuv_bd)


def _merge_kernel(*refs, n_prompt_tiles, with_router):
    if with_router:
        (oap_ref, obp_ref, oas_ref, obs_ref, g_ref, h_ref, wa_ref, wb_ref, wo_ref, gf_ref, wr_ref,
         hn_ref, xn_ref, gates_ref) = refs
    else:
        (oap_ref, obp_ref, oas_ref, obs_ref, g_ref, h_ref, wa_ref, wb_ref, wo_ref, gf_ref,
         hn_ref, xn_ref) = refs
    is_prompt = pl.program_id(0) < n_prompt_tiles
    oa = jnp.where(is_prompt, oap_ref[...], oas_ref[...])
    ob = jnp.where(is_prompt, obp_ref[...], obs_ref[...])
    br_a = _mm(oa, wa_ref[...])
    br_b = _mm(ob, wb_ref[...])
    mixed = g_ref[:, 0:D_MODEL].astype(F32) * br_a + g_ref[:, D_MODEL:2 * D_MODEL].astype(F32) * br_b
    hn = h_ref[...] + _mm(mixed.astype(BF16), wo_ref[...])
    hn_ref[...] = hn
    xn = hn * lax.rsqrt(jnp.mean(hn * hn, axis=-1, keepdims=True) + EPS) * gf_ref[...]
    xn_ref[...] = xn.astype(BF16)
    if with_router:
        logits = jnp.dot(xn, wr_ref[...], preferred_element_type=F32, precision=lax.Precision.HIGHEST)
        lane = lax.broadcasted_iota(jnp.int32, logits.shape, 1)
        valid = lane < N_EXPERTS
        lg = jnp.where(valid, logits, NEG_INF)
        m1 = jnp.max(lg, axis=1, keepdims=True)
        i1 = jnp.min(jnp.where(lg == m1, lane, LANES), axis=1, keepdims=True)
        lg2 = jnp.where(lane == i1, NEG_INF, lg)
        m2 = jnp.max(lg2, axis=1, keepdims=True)
        i2 = jnp.min(jnp.where(lg2 == m2, lane, LANES), axis=1, keepdims=True)
        e2 = jnp.exp(m2 - m1)
        w1 = 1.0 / (1.0 + e2)
        w2 = e2 / (1.0 + e2)
        gates_ref[...] = jnp.where(lane == i1, w1, 0.0) + jnp.where(lane == i2, w2, 0.0)


def _merge_call(oa_p, ob_p, oa_s, ob_s, gates, h, wa, wb, wo, gf, wr, tm):
    T = h.shape[0]
    n_p = oa_p.shape[0] // tm
    with_router = wr is not None
    row = lambda w: pl.BlockSpec((tm, w), lambda i: (i, 0))
    full = lambda a: pl.BlockSpec(a.shape, lambda i: (0, 0))
    p_spec = pl.BlockSpec((tm, oa_p.shape[1]), lambda i: (jnp.minimum(i, n_p - 1), 0))
    s_spec = pl.BlockSpec((tm, oa_s.shape[1]), lambda i: (jnp.maximum(i - n_p, 0), 0))
    in_specs = [p_spec, p_spec, s_spec, s_spec, row(2 * D_MODEL), row(D_MODEL), full(wa), full(wb), full(wo), full(gf)]
    operands = [oa_p, ob_p, oa_s, ob_s, gates, h, wa, wb, wo, gf]
    out_shape = [jax.ShapeDtypeStruct((T, D_MODEL), F32), jax.ShapeDtypeStruct((T, D_MODEL), BF16)]
    out_specs = [row(D_MODEL), row(D_MODEL)]
    if with_router:
        in_specs.append(full(wr))
        operands.append(wr)
        out_shape.append(jax.ShapeDtypeStruct((T, LANES), F32))
        out_specs.append(row(LANES))
    return pl.pallas_call(
        functools.partial(_merge_kernel, n_prompt_tiles=n_p, with_router=with_router),
        grid=(T // tm,),
        in_specs=in_specs,
        out_specs=tuple(out_specs),
        out_shape=tuple(out_shape),
        compiler_params=_cparams(("parallel",)),
        name="merge",
    )(*operands)


def _swiglu_tile(x, wg, wu):
    a = _mm(x, wg)
    return (a * (1.0 / (1.0 + jnp.exp(-a))) * _mm(x, wu)).astype(BF16)


def _ffn_kernel(x_ref, h_ref, wg_ref, wu_ref, wd_ref, o_ref, acc_scr):
    f = pl.program_id(1)

    @pl.when(f == 0)
    def _():
        acc_scr[...] = h_ref[...]

    acc_scr[...] += _mm(_swiglu_tile(x_ref[...], wg_ref[...], wu_ref[...]), wd_ref[...])

    @pl.when(f == pl.num_programs(1) - 1)
    def _():
        o_ref[...] = acc_scr[...]


def _ffn_call(xn, h, wg, wu, wd, tm, tf):
    T = h.shape[0]
    d_ff = wg.shape[1]
    return pl.pallas_call(
        _ffn_kernel,
        grid=(T // tm, d_ff // tf),
        in_specs=[
            pl.BlockSpec((tm, D_MODEL), lambda i, f: (i, 0)),
            pl.BlockSpec((tm, D_MODEL), lambda i, f: (i, 0)),
            pl.BlockSpec((D_MODEL, tf), lambda i, f: (0, f)),
            pl.BlockSpec((D_MODEL, tf), lambda i, f: (0, f)),
            pl.BlockSpec((tf, D_MODEL), lambda i, f: (f, 0)),
        ],
        out_specs=pl.BlockSpec((tm, D_MODEL), lambda i, f: (i, 0)),
        out_shape=jax.ShapeDtypeStruct((T, D_MODEL), F32),
        scratch_shapes=[pltpu.VMEM((tm, D_MODEL), F32)],
        compiler_params=_cparams(("parallel", "arbitrary")),
        name="ffn_dense",
    )(xn, h, wg, wu, wd)


def _moe_kernel(x_ref, h_ref, gates_ref, wg_ref, wu_ref, wd_ref, o_ref, acc_scr, eacc_scr):
    e = pl.program_id(1)
    f = pl.program_id(2)
    last_f = f == pl.num_programs(2) - 1

    @pl.when((e == 0) & (f == 0))
    def _():
        acc_scr[...] = h_ref[...]

    y = _mm(_swiglu_tile(x_ref[...], wg_ref[...], wu_ref[...]), wd_ref[...])

    @pl.when(f == 0)
    def _():
        eacc_scr[...] = y

    @pl.when(f > 0)
    def _():
        eacc_scr[...] += y

    @pl.when(last_f)
    def _():
        lane = lax.broadcasted_iota(jnp.int32, gates_ref.shape, 1)
        gate = jnp.sum(jnp.where(lane == e, gates_ref[...], 0.0), axis=1, keepdims=True)
        acc_scr[...] += gate * eacc_scr[...]

    @pl.when(last_f & (e == pl.num_programs(1) - 1))
    def _():
        o_ref[...] = acc_scr[...]


def _moe_call(xn, h, gates, wg, wu, wd, tm, tf):
    T = h.shape[0]
    n_e, _, d_ff = wg.shape
    return pl.pallas_call(
        _moe_kernel,
        grid=(T // tm, n_e, d_ff // tf),
        in_specs=[
            pl.BlockSpec((tm, D_MODEL), lambda i, e, f: (i, 0)),
            pl.BlockSpec((tm, D_MODEL), lambda i, e, f: (i, 0)),
            pl.BlockSpec((tm, LANES), lambda i, e, f: (i, 0)),
            pl.BlockSpec((None, D_MODEL, tf), lambda i, e, f: (e, 0, f)),
            pl.BlockSpec((None, D_MODEL, tf), lambda i, e, f: (e, 0, f)),
            pl.BlockSpec((None, tf, D_MODEL), lambda i, e, f: (e, f, 0)),
        ],
        out_specs=pl.BlockSpec((tm, D_MODEL), lambda i, e, f: (i, 0)),
        out_shape=jax.ShapeDtypeStruct((T, D_MODEL), F32),
        scratch_shapes=[pltpu.VMEM((tm, D_MODEL), F32), pltpu.VMEM((tm, D_MODEL), F32)],
        compiler_params=_cparams(("parallel", "arbitrary", "arbitrary")),
        name="ffn_experts",
    )(xn, h, gates, wg, wu, wd)


def _head_block_cols(w_rope, w_nope):
    pad = jnp.zeros(w_rope.shape[:-1] + (LANES - MLA_ROPE - MLA_NOPE,), w_rope.dtype)
    blk = jnp.concatenate([w_rope, w_nope, pad], axis=-1)
    return blk.reshape(blk.shape[:-2] + (blk.shape[-2] * LANES,))


def _frame(rope=None, nope=None):
    v = jnp.zeros((LANES,), F32)
    if rope is not None:
        v = v.at[0:MLA_ROPE].set(rope)
    if nope is not None:
        v = v.at[NOPE_LO:NOPE_HI].set(nope)
    return v


def _row(v):
    return jnp.zeros((D_MODEL,), F32).at[0:v.shape[0]].set(v)


def _rope_table(pos):
    inv_freq = ROPE_THETA ** (-jnp.arange(HALF_ROPE, dtype=F32) / HALF_ROPE)
    ang = pos.astype(F32)[:, None] * inv_freq[None, :]
    cos, sin = jnp.cos(ang), jnp.sin(ang)
    n = pos.shape[0]
    z16 = jnp.zeros((n, HALF_ROPE), F32)
    cos_t = jnp.concatenate([cos, cos, jnp.ones((n, LANES - MLA_ROPE), F32)], axis=1)
    sina = jnp.concatenate([-sin, jnp.zeros((n, LANES - HALF_ROPE), F32)], axis=1)
    sinb = jnp.concatenate([z16, sin, jnp.zeros((n, LANES - MLA_ROPE), F32)], axis=1)
    return jnp.concatenate([cos_t, sina, sinb], axis=1)


def kernel(x_prompt, x_sample, cache_mla_latent, cache_mla_krope, cache_sb_k, cache_sb_v, page_table, g_attn_norm, w_in, g_q_lat, w_q_up, g_q_nope, g_q_rope, g_kv_lat, g_k_rope, w_uk, g_k_nope, w_uv, w_mla_out, w_sb_out, w_o, g_ffn_norm, w_ffn_gate, w_ffn_up, w_ffn_down, w_router, w_exp_gate, w_exp_up, w_exp_down):
    batch, seq, _ = x_prompt.shape
    dec_batch, dec_seq, _ = x_sample.shape
    depth = w_in.shape[0]
    n_pages = page_table.shape[1]
    page = cache_mla_latent.shape[2]
    past_len = n_pages * page
    t_p = batch * seq
    t_s = dec_batch * dec_seq
    T = t_p + t_s
    assert page == LANES and dec_seq <= page

    tm = _pick(math.gcd(t_p, t_s), (512, 256, 128, 64, 32, 16))
    tq = _pick(seq, (512, 256, 128))
    ck = min(SB_CHUNK, tq)
    n_pg = _pick(n_pages, (8, 4, 2, 1))
    tm_ffn = _pick(T, (768, 512, 384, 256, 192, 128, 64, 32, 16))

    pos = jnp.concatenate([
        jnp.tile(jnp.arange(seq, dtype=jnp.int32), batch),
        jnp.tile(past_len + jnp.arange(dec_seq, dtype=jnp.int32), dec_batch)])
    tab = _rope_table(pos)
    tt_prompt = _suffix_matrix(ck)
    tt_page = _suffix_matrix(page)

    cache_krt = jnp.transpose(cache_mla_krope, (0, 1, 3, 2))
    cache_kt = jnp.transpose(cache_sb_k, (0, 1, 3, 4, 2))
    cache_vt = jnp.transpose(cache_sb_v, (0, 1, 3, 4, 2))

    h = jnp.concatenate([x_prompt.reshape(t_p, D_MODEL), x_sample.reshape(t_s, D_MODEL)], axis=0)
    outs = [[] for _ in range(8)]
    sizes = np.cumsum([Q_LORA, KV_LORA, MLA_ROPE, SB_HEADS * SB_HEAD_DIM, SB_KV_HEADS * SB_HEAD_DIM,
                       SB_KV_HEADS * SB_HEAD_DIM, D_MODEL])

    for l in range(depth):
        w_ql, w_c, w_kr, w_sq, w_sk, w_sv, w_ga, w_gb = jnp.split(w_in[l], sizes, axis=1)
        w_kr_pad = jnp.concatenate([w_kr, jnp.zeros((D_MODEL, LANES - MLA_ROPE), F32)], axis=1)
        w_all = jnp.concatenate([w_ql, w_c, w_kr_pad, w_sq * SB_SCALE, w_sk, w_sv, w_ga, w_gb], axis=1).astype(BF16)
        wq3 = w_q_up[l].reshape(Q_LORA, MLA_HEADS, MLA_NOPE + MLA_ROPE)
        wqu = _head_block_cols(wq3[..., MLA_NOPE:], wq3[..., :MLA_NOPE]).astype(BF16)
        wuk = _head_block_cols(jnp.zeros((KV_LORA, MLA_HEADS, MLA_ROPE), F32), w_uk[l]).astype(BF16)
        gv = jnp.stack([
            g_attn_norm[l], _row(g_q_lat[l]),
            _row(_frame(rope=g_q_rope[l] * MLA_SCALE)), _row(_frame(nope=g_q_nope[l] * MLA_SCALE)),
            _row(g_kv_lat[l]), _row(_frame(rope=g_k_rope[l])), _row(_frame(nope=g_k_nope[l])),
            jnp.zeros((D_MODEL,), F32)])
        wuv_bd = jnp.zeros((MLA_HEADS * KV_LORA, MLA_HEADS * MLA_V), F32)
        for hh in range(MLA_HEADS):
            wuv_bd = wuv_bd.at[hh * KV_LORA:(hh + 1) * KV_LORA, hh * MLA_V:(hh + 1) * MLA_V].set(w_uv[l][:, hh, :])
        wuv_bd = wuv_bd.astype(BF16)

        (qcat, kcat, c, cbf, kr, sq, sk, sv, skbf, svbf, gates) = _proj_call(h, tab, gv, w_all, wqu, wuk, tm)

        oa_p = _mla_call(qcat, kcat, cbf, wuv_bd, batch, seq, tq)
        k_heads = skbf[:t_p].reshape(batch, seq, SB_KV_HEADS, SB_HEAD_DIM).transpose(0, 2, 1, 3)
        v_heads = svbf[:t_p].reshape(batch, seq, SB_KV_HEADS, SB_HEAD_DIM).transpose(0, 2, 1, 3)
        v_heads = jnp.concatenate([v_heads, v_heads], axis=-1)
        ob_p = _sb_call(sq, k_heads, v_heads, tt_prompt, batch, seq, tq, ck)

        q_s = qcat[t_p:]
        wabs = jnp.zeros((MLA_HEADS, LANES, KV_LORA), F32)
        wabs = wabs.at[:, NOPE_LO:NOPE_HI, :].set(
            jnp.transpose(w_uk[l] * g_k_nope[l][None, None, :], (1, 2, 0)))
        qabs = _absorb_call(q_s, wabs.astype(BF16))
        qabs = qabs.reshape(dec_batch, dec_seq * MLA_HEADS, KV_LORA)
        qr = q_s.reshape(dec_batch, dec_seq * MLA_HEADS, LANES)[:, :, 0:MLA_ROPE]
        sq_s = sq[t_p:].reshape(dec_batch, dec_seq, SB_HEADS, SB_HEAD_DIM).transpose(0, 2, 1, 3)
        sq_s = sq_s.reshape(dec_batch, SB_KV_HEADS, SB_GROUP * dec_seq, SB_HEAD_DIM)
        eye = jnp.eye(SB_KV_HEADS, dtype=BF16)
        qbd = (sq_s[:, :, :, None, :] * eye[None, :, None, :, None]).reshape(
            dec_batch, SB_HEADS * dec_seq, SB_KV_HEADS * SB_HEAD_DIM)
        wukt = jnp.transpose(w_uk[l], (1, 2, 0)).reshape(MLA_HEADS * MLA_NOPE, KV_LORA).astype(BF16)
        pad_k = page - dec_seq
        nlat = jnp.pad(c[t_p:].reshape(dec_batch, dec_seq, KV_LORA), ((0, 0), (0, pad_k), (0, 0)))
        nkr = jnp.pad(kr[t_p:].reshape(dec_batch, dec_seq, MLA_ROPE).transpose(0, 2, 1), ((0, 0), (0, 0), (0, pad_k)))
        nk = jnp.pad(sk[t_p:].reshape(dec_batch, dec_seq, -1).transpose(0, 2, 1), ((0, 0), (0, 0), (0, pad_k)))
        nv = jnp.pad(sv[t_p:].reshape(dec_batch, dec_seq, -1).transpose(0, 2, 1), ((0, 0), (0, 0), (0, pad_k)))
        o_lat_s, o_sbt = _sample_call(l, page_table, qabs, qr, qbd, wukt, tt_page, nlat, nkr, nk, nv,
                                      cache_mla_latent, cache_krt, cache_kt, cache_vt, n_pg)
        o_lat_s = o_lat_s.reshape(dec_batch, dec_seq, MLA_HEADS, KV_LORA).reshape(t_s, MLA_HEADS * KV_LORA)
        oa_s = _uv_call(o_lat_s, wuv_bd)
        o5 = o_sbt.reshape(dec_batch, SB_KV_HEADS, SB_HEAD_DIM, SB_KV_HEADS, SB_GROUP, dec_seq)
        o5 = jnp.stack([o5[:, kv, :, kv] for kv in range(SB_KV_HEADS)], axis=1)
        ob_s = o5.transpose(0, 4, 1, 3, 2).reshape(t_s, SB_HEADS * SB_HEAD_DIM).astype(BF16)

        j = l // 2
        moe = l % 2 == 1
        wr = None
        if moe:
            wr = jnp.concatenate([w_router[j], jnp.zeros((D_MODEL, LANES - N_EXPERTS), F32)], axis=1)
        res = _merge_call(oa_p, ob_p, oa_s, ob_s, gates, h,
                          w_mla_out[l].astype(BF16), w_sb_out[l].astype(BF16), w_o[l].astype(BF16),
                          g_ffn_norm[l][None, :], wr, tm)
        if moe:
            hn, xn, route = res
            h = _moe_call(xn, hn, route, w_exp_gate[j].astype(BF16), w_exp_up[j].astype(BF16),
                          w_exp_down[j].astype(BF16), tm_ffn, 512)
        else:
            hn, xn = res
            d_ff = w_ffn_gate.shape[2]
            h = _ffn_call(xn, hn, w_ffn_gate[j].astype(BF16), w_ffn_up[j].astype(BF16),
                          w_ffn_down[j].astype(BF16), tm_ffn, _pick(d_ff, (1408, 512, 256, 128)))

        for idx, a in enumerate((c, kr, sk, sv)):
            outs[idx].append(a[:t_p])
            outs[4 + idx].append(a[t_p:])

    def stack_p(xs, tail):
        return jnp.stack(xs).reshape((depth, batch, seq) + tail)

    def stack_s(xs, tail):
        return jnp.stack(xs).reshape((depth, dec_batch, dec_seq) + tail)

    kv_tail = (SB_KV_HEADS, SB_HEAD_DIM)
    return (h[:t_p].reshape(batch, seq, D_MODEL), h[t_p:].reshape(dec_batch, dec_seq, D_MODEL),
            stack_p(outs[0], (KV_LORA,)), stack_p(outs[1], (MLA_ROPE,)),
            stack_p(outs[2], kv_tail), stack_p(outs[3], kv_tail),
            stack_s(outs[4], (KV_LORA,)), stack_s(outs[5], (MLA_ROPE,)),
            stack_s(outs[6], kv_tail), stack_s(outs[7], kv_tail))
```

```python
import functools
import math

import jax
import jax.numpy as jnp
import numpy as np
from jax import lax
from jax.experimental import pallas as pl
from jax.experimental.pallas import tpu as pltpu

D_MODEL = 1024
MLA_HEADS = 8
MLA_NOPE = 64
MLA_ROPE = 32
MLA_V = 64
Q_LORA = 256
KV_LORA = 128
ROPE_THETA = 10000.0
MLA_SCALE = (MLA_NOPE + MLA_ROPE) ** -0.5
SB_HEADS = 8
SB_KV_HEADS = 4
SB_GROUP = SB_HEADS // SB_KV_HEADS
SB_HEAD_DIM = 64
SB_SCALE = SB_HEAD_DIM ** -0.5
N_EXPERTS = 8
EPS = 1e-6
NEG_INF = -1e30

LANES = 128
HALF_ROPE = MLA_ROPE // 2
NOPE_LO, NOPE_HI = MLA_ROPE, MLA_ROPE + MLA_NOPE
VMEM_LIMIT = 56 * 1024 * 1024

F32 = jnp.float32
BF16 = jnp.bfloat16


def _pick(n, candidates):
    for c in candidates:
        if n % c == 0:
            return c
    return n


def _mm(a, b):
    return jnp.dot(a, b, preferred_element_type=F32)


def _mm_nt(a, b):
    return lax.dot_general(a, b, (((1,), (1,)), ((), ())), preferred_element_type=F32)


def _cparams(sem):
    return pltpu.CompilerParams(dimension_semantics=sem, vmem_limit_bytes=VMEM_LIMIT)


def _staggered(gens, offset=1):
    pending = list(gens)
    active = []
    rnd = 0
    while pending or active:
        if pending and (offset == 0 or rnd % offset == 0):
            if offset == 0:
                active.extend(pending)
                pending = []
            else:
                active.append(pending.pop(0))
        for g in list(active):
            try:
                next(g)
            except StopIteration:
                active.remove(g)
        rnd += 1


_C_QLAT, _C_CKV, _C_KR, _C_SQ, _C_SK, _C_SV, _C_G, _C_END = 0, 256, 384, 512, 1024, 1280, 1536, 3584


def _rope_rotate(y, cos, sina, sinb):
    return y * cos + pltpu.roll(y, LANES - HALF_ROPE, 1) * sina + pltpu.roll(y, HALF_ROPE, 1) * sinb


def _proj_kernel(h_ref, tab_ref, gv_ref, w_ref, wqu_ref, wuk_ref,
                 qcat_ref, kcat_ref, c_ref, cbf_ref, kr_ref, sq_ref, sk_ref, sv_ref,
                 skbf_ref, svbf_ref, gates_ref):
    x = h_ref[...]
    g_attn = gv_ref[0:1, :]
    xn = (x * lax.rsqrt(jnp.mean(x * x, axis=-1, keepdims=True) + EPS) * g_attn).astype(BF16)

    cos = tab_ref[:, 0:LANES]
    sina = tab_ref[:, LANES:2 * LANES]
    sinb = tab_ref[:, 2 * LANES:3 * LANES]
    lane = lax.broadcasted_iota(jnp.int32, (1, LANES), 1)
    m_rope = (lane < MLA_ROPE).astype(F32)
    m_nope = ((lane >= NOPE_LO) & (lane < NOPE_HI)).astype(F32)

    q_lat = _mm(xn, w_ref[:, _C_QLAT:_C_CKV])
    g_ql = gv_ref[1:2, 0:Q_LORA]
    ql = (q_lat * lax.rsqrt(jnp.mean(q_lat * q_lat, axis=-1, keepdims=True) + EPS) * g_ql).astype(BF16)
    q = _mm(ql, wqu_ref[...])
    g_qr = gv_ref[2:3, 0:LANES]
    g_qn = gv_ref[3:4, 0:LANES]
    for h in range(MLA_HEADS):
        xh = q[:, h * LANES:(h + 1) * LANES]
        x2 = xh * xh
        ss_r = jnp.sum(x2 * m_rope, axis=-1, keepdims=True)
        ss_n = jnp.sum(x2 * m_nope, axis=-1, keepdims=True)
        sc = lax.rsqrt(ss_r * (1.0 / MLA_ROPE) + EPS) * g_qr + lax.rsqrt(ss_n * (1.0 / MLA_NOPE) + EPS) * g_qn
        y = _rope_rotate(xh * sc, cos, sina, sinb)
        qcat_ref[:, h * LANES:(h + 1) * LANES] = y.astype(BF16)

    c_raw = _mm(xn, w_ref[:, _C_CKV:_C_KR])
    g_kvl = gv_ref[4:5, 0:KV_LORA]
    c = c_raw * lax.rsqrt(jnp.mean(c_raw * c_raw, axis=-1, keepdims=True) + EPS) * g_kvl
    c_ref[...] = c
    cbf = c.astype(BF16)
    cbf_ref[...] = cbf

    kr_raw = _mm(xn, w_ref[:, _C_KR:_C_SQ])
    g_kr = gv_ref[5:6, 0:LANES]
    ss = jnp.sum(kr_raw * kr_raw, axis=-1, keepdims=True)
    krn = _rope_rotate(kr_raw * (lax.rsqrt(ss * (1.0 / MLA_ROPE) + EPS) * g_kr), cos, sina, sinb)
    kr_ref[...] = krn[:, 0:MLA_ROPE]

    kp = _mm(cbf, wuk_ref[...])
    g_kn = gv_ref[6:7, 0:LANES]
    for h in range(MLA_HEADS):
        kh = kp[:, h * LANES:(h + 1) * LANES]
        ss_n = jnp.sum(kh * kh, axis=-1, keepdims=True)
        kn = kh * (lax.rsqrt(ss_n * (1.0 / MLA_NOPE) + EPS) * g_kn)
        kcat_ref[:, h * LANES:(h + 1) * LANES] = (kn + krn).astype(BF16)

    sq_ref[...] = _mm(xn, w_ref[:, _C_SQ:_C_SK]).astype(BF16)
    sk = _mm(xn, w_ref[:, _C_SK:_C_SV])
    sk_ref[...] = sk
    skbf_ref[...] = sk.astype(BF16)
    sv = _mm(xn, w_ref[:, _C_SV:_C_G])
    sv_ref[...] = sv
    svbf_ref[...] = sv.astype(BF16)

    g = _mm(xn, w_ref[:, _C_G:_C_END])
    gates_ref[...] = (1.0 / (1.0 + jnp.exp(-g))).astype(BF16)


def _proj_call(h, tab, gv, w_all, wqu, wuk, tm):
    T = h.shape[0]
    row = lambda w: pl.BlockSpec((tm, w), lambda i: (i, 0))
    full = lambda a: pl.BlockSpec(a.shape, lambda i: (0, 0))
    out_shapes = (
        jax.ShapeDtypeStruct((T, MLA_HEADS * LANES), BF16),
        jax.ShapeDtypeStruct((T, MLA_HEADS * LANES), BF16),
        jax.ShapeDtypeStruct((T, KV_LORA), F32),
        jax.ShapeDtypeStruct((T, KV_LORA), BF16),
        jax.ShapeDtypeStruct((T, MLA_ROPE), F32),
        jax.ShapeDtypeStruct((T, SB_HEADS * SB_HEAD_DIM), BF16),
        jax.ShapeDtypeStruct((T, SB_KV_HEADS * SB_HEAD_DIM), F32),
        jax.ShapeDtypeStruct((T, SB_KV_HEADS * SB_HEAD_DIM), F32),
        jax.ShapeDtypeStruct((T, SB_KV_HEADS * SB_HEAD_DIM), BF16),
        jax.ShapeDtypeStruct((T, SB_KV_HEADS * SB_HEAD_DIM), BF16),
        jax.ShapeDtypeStruct((T, 2 * D_MODEL), BF16),
    )
    return pl.pallas_call(
        _proj_kernel,
        grid=(T // tm,),
        in_specs=[row(D_MODEL), row(3 * LANES), full(gv), full(w_all), full(wqu), full(wuk)],
        out_specs=tuple(row(s.shape[1]) for s in out_shapes),
        out_shape=out_shapes,
        compiler_params=_cparams(("parallel",)),
        name="proj",
    )(h, tab, gv, w_all, wqu, wuk)


def _mla_kernel(qi_tab, ki_tab, q_ref, k_ref, v_ref, wuv_ref, o_ref, m_scr, l_scr, acc_scr, *, tq):
    t = pl.program_id(1)
    qi = qi_tab[t]
    ki = ki_tab[t]

    @pl.when(ki == 0)
    def _():
        m_scr[...] = jnp.full(m_scr.shape, NEG_INF, F32)
        l_scr[...] = jnp.zeros(l_scr.shape, F32)
        acc_scr[...] = jnp.zeros(acc_scr.shape, F32)

    def step(masked):
        v = v_ref[...]
        if masked:
            row = lax.broadcasted_iota(jnp.int32, (tq, tq), 0)
            col = lax.broadcasted_iota(jnp.int32, (tq, tq), 1)
            keep = col <= row

        for h in range(MLA_HEADS):
            s = _mm_nt(q_ref[:, h * LANES:(h + 1) * LANES], k_ref[:, h * LANES:(h + 1) * LANES])
            if masked:
                s = jnp.where(keep, s, NEG_INF)
            m_prev = m_scr[h]
            m_new = jnp.maximum(m_prev, jnp.max(s, axis=1, keepdims=True))
            alpha = jnp.exp(m_prev - m_new)
            p = jnp.exp(s - m_new[:, 0:1])
            l_scr[h] = alpha * l_scr[h] + jnp.sum(p, axis=1, keepdims=True)
            acc_scr[h] = alpha * acc_scr[h] + _mm(p.astype(BF16), v)
            m_scr[h] = m_new

    @pl.when(ki < qi)
    def _():
        step(False)

    @pl.when(ki == qi)
    def _():
        step(True)
        o_lat = jnp.concatenate([(acc_scr[h] / l_scr[h]).astype(BF16) for h in range(MLA_HEADS)], axis=1)
        o_ref[...] = _mm(o_lat, wuv_ref[...]).astype(BF16)


def _tri_tables(nq, descending):
    qi, ki = [], []
    for i in range(nq):
        ks = range(i, -1, -1) if descending else range(i + 1)
        for j in ks:
            qi.append(i)
            ki.append(j)
    return jnp.asarray(np.array(qi, np.int32)), jnp.asarray(np.array(ki, np.int32))


def _mla_call(qcat, kcat, cbf, wuv_bd, batch, seq, tq):
    nq = seq // tq
    qi_tab, ki_tab = _tri_tables(nq, descending=False)
    n_tri = int(qi_tab.shape[0])
    return pl.pallas_call(
        functools.partial(_mla_kernel, tq=tq),
        grid_spec=pltpu.PrefetchScalarGridSpec(
            num_scalar_prefetch=2,
            grid=(batch, n_tri),
            in_specs=[
                pl.BlockSpec((tq, MLA_HEADS * LANES), lambda b, t, qt, kt: (b * nq + qt[t], 0)),
                pl.BlockSpec((tq, MLA_HEADS * LANES), lambda b, t, qt, kt: (b * nq + kt[t], 0)),
                pl.BlockSpec((tq, KV_LORA), lambda b, t, qt, kt: (b * nq + kt[t], 0)),
                pl.BlockSpec(wuv_bd.shape, lambda b, t, qt, kt: (0, 0)),
            ],
            out_specs=pl.BlockSpec((tq, MLA_HEADS * MLA_V), lambda b, t, qt, kt: (b * nq + qt[t], 0)),
            scratch_shapes=[
                pltpu.VMEM((MLA_HEADS, tq, LANES), F32),
                pltpu.VMEM((MLA_HEADS, tq, LANES), F32),
                pltpu.VMEM((MLA_HEADS, tq, KV_LORA), F32),
            ],
        ),
        out_shape=jax.ShapeDtypeStruct((batch * seq, MLA_HEADS * MLA_V), BF16),
        compiler_params=_cparams(("parallel", "arbitrary")),
        name="mla_prompt",
    )(qi_tab, ki_tab, qcat, kcat, cbf, wuv_bd)


SB_CHUNK = 256


def _suffix_matrix(n):
    j = np.arange(n)[:, None]
    s = np.arange(n)[None, :]
    t = (j > s).astype(np.float32)
    return jnp.asarray(np.concatenate([t, t], axis=0), dtype=BF16)


def _sb_terms(z):
    lp = jnp.log(1.0 + jnp.exp(-jnp.abs(z)))
    log_beta = jnp.minimum(z, 0.0) - lp
    return log_beta, log_beta - z


def _split_bf16(x):
    hi = x.astype(BF16)
    return hi, (x - hi.astype(F32)).astype(BF16)


SB_DEAD_LOG = -104.0


def _sb_chunk(q2, k, v, tt, carry, keep, tq):
    res = []

    def gen():
        z = _mm_nt(q2, k)
        yield
        log_beta, l1 = _sb_terms(z)
        if keep is not None:
            l1 = jnp.where(keep, l1, 0.0)
        hi, lo = _split_bf16(l1)
        res.append(carry + jnp.sum(l1, axis=1, keepdims=True))
        yield
        after = _mm(jnp.concatenate([hi, lo], axis=1), tt)
        yield
        w = jnp.exp(log_beta + after + carry[:, 0:1])
        if keep is not None:
            w = jnp.where(keep, w, 0.0)
        w = w.astype(BF16)
        yield
        o2 = _mm(w, v)
        first_head = lax.broadcasted_iota(jnp.int32, (1, LANES), 1) < SB_HEAD_DIM
        res.append(jnp.where(first_head, o2[0:tq], o2[tq:2 * tq]))

    return gen(), res


def _sb_kernel(q_ref, k_ref, v_ref, tt_ref, o_ref, carry_scr, acc_scr, *, tq, ck):
    qi = pl.program_id(2)
    n_chunks = tq // ck
    tt = tt_ref[...]
    q2 = jnp.concatenate([q_ref[:, g * SB_HEAD_DIM:(g + 1) * SB_HEAD_DIM] for g in range(SB_GROUP)], axis=0)

    base = pl.multiple_of(qi * tq, tq)
    row = lax.broadcasted_iota(jnp.int32, (SB_GROUP * tq, ck), 0) & (tq - 1)
    col0 = lax.broadcasted_iota(jnp.int32, (SB_GROUP * tq, ck), 1)
    carry = jnp.zeros((SB_GROUP * tq, LANES), F32)
    gens, results = [], []
    carry_box = [carry]

    def diag_chunk(c):
        k = k_ref[pl.ds(base + c * ck, ck), :]
        v = v_ref[pl.ds(base + c * ck, ck), :]
        z = _mm_nt(q2, k)
        yield
        keep = (col0 + c * ck) < row
        log_beta, l1 = _sb_terms(z)
        l1 = jnp.where(keep, l1, 0.0)
        hi, lo = _split_bf16(l1)
        carry_in = carry_box[-1]
        carry_box.append(carry_in + jnp.sum(l1, axis=1, keepdims=True))
        yield
        after = _mm(jnp.concatenate([hi, lo], axis=1), tt)
        yield
        w = jnp.where(keep, jnp.exp(log_beta + after + carry_in[:, 0:1]), 0.0).astype(BF16)
        yield
        o2 = _mm(w, v)
        first_head = lax.broadcasted_iota(jnp.int32, (1, LANES), 1) < SB_HEAD_DIM
        results.append(jnp.where(first_head, o2[0:tq], o2[tq:2 * tq]))

    _staggered([diag_chunk(c) for c in range(n_chunks - 1, -1, -1)])
    carry_scr[...] = carry_box[-1]
    acc_scr[...] = functools.reduce(lambda a, b: a + b, results)

    def alive_of(c):
        return (jnp.max(c) >= SB_DEAD_LOG).astype(jnp.int32)

    def cond(st):
        j, alive = st
        return (j >= 0) & (alive > 0)

    def body(st):
        j, _ = st
        start = pl.multiple_of(j * ck, ck)
        gen, res = _sb_chunk(q2, k_ref[pl.ds(start, ck), :], v_ref[pl.ds(start, ck), :], tt,
                             carry_scr[...], None, tq)
        for _ in gen:
            pass
        carry_new, out = res
        carry_scr[...] = carry_new
        acc_scr[...] += out
        return j - 1, alive_of(carry_new)

    lax.while_loop(cond, body, (qi * n_chunks - 1, alive_of(carry_box[-1])))
    o_ref[...] = acc_scr[...].astype(BF16)


def _sb_call(sq, k_heads, v_heads, tt, batch, seq, tq, ck):
    nq = seq // tq
    gw = SB_GROUP * SB_HEAD_DIM
    return pl.pallas_call(
        functools.partial(_sb_kernel, tq=tq, ck=ck),
        grid=(batch, SB_KV_HEADS, nq),
        in_specs=[
            pl.BlockSpec((tq, gw), lambda b, kv, i: (b * nq + i, kv)),
            pl.BlockSpec((None, None, seq, SB_HEAD_DIM), lambda b, kv, i: (b, kv, 0, 0)),
            pl.BlockSpec((None, None, seq, gw), lambda b, kv, i: (b, kv, 0, 0)),
            pl.BlockSpec(tt.shape, lambda b, kv, i: (0, 0)),
        ],
        out_specs=pl.BlockSpec((tq, gw), lambda b, kv, i: (b * nq + i, kv)),
        scratch_shapes=[
            pltpu.VMEM((SB_GROUP * tq, LANES), F32),
            pltpu.VMEM((tq, gw), F32),
        ],
        out_shape=jax.ShapeDtypeStruct((batch * seq, SB_HEADS * SB_HEAD_DIM), BF16),
        compiler_params=_cparams(("parallel", "parallel", "arbitrary")),
        name="sb_prompt",
    )(sq, k_heads, v_heads, tt)


def _absorb_kernel(q_ref, w_ref, o_ref):
    for h in range(MLA_HEADS):
        o_ref[:, h * LANES:(h + 1) * LANES] = _mm(q_ref[:, h * LANES:(h + 1) * LANES], w_ref[h]).astype(BF16)


def _absorb_call(q_s, wabs):
    n = q_s.shape[0]
    return pl.pallas_call(
        _absorb_kernel,
        grid=(1,),
        in_specs=[pl.BlockSpec(q_s.shape, lambda i: (0, 0)), pl.BlockSpec(wabs.shape, lambda i: (0, 0, 0))],
        out_specs=pl.BlockSpec((n, MLA_HEADS * LANES), lambda i: (0, 0)),
        out_shape=jax.ShapeDtypeStruct((n, MLA_HEADS * LANES), BF16),
        compiler_params=_cparams(("arbitrary",)),
        name="absorb_q",
    )(q_s, wabs)


def _sample_sb_tile(qbd, tt, kt_pages, vt_pages, carry_in, keep, n_sb):
    res = []

    def gen():
        n = len(kt_pages)
        z = jnp.concatenate([_mm(qbd, p.astype(BF16)) for p in kt_pages], axis=0)
        yield
        log_beta, l1 = _sb_terms(z)
        if keep is not None:
            l1 = jnp.where(keep, l1, 0.0)
        hi, lo = _split_bf16(l1)
        tot = jnp.sum(l1, axis=1, keepdims=True)
        run = carry_in
        carries = [None] * n
        for i in range(n - 1, -1, -1):
            carries[i] = run
            run = run + tot[i * n_sb:(i + 1) * n_sb]
        res.append(run)
        carry = jnp.concatenate([c[:, 0:1] for c in carries], axis=0)
        yield
        after = _mm(jnp.concatenate([hi, lo], axis=1), tt)
        yield
        w = jnp.exp(log_beta + after + carry)
        if keep is not None:
            w = jnp.where(keep, w, 0.0)
        wl = jnp.concatenate([w[i * n_sb:(i + 1) * n_sb].astype(BF16) for i in range(n)], axis=1)
        vt = jnp.concatenate([p.astype(BF16) for p in vt_pages], axis=1)
        res.append(_mm_nt(vt, wl))

    return gen(), res


def _sample_kernel(*refs, n_pg, n_head, dec_seq, page):
    pt_ref = refs[0]
    del pt_ref
    qabs_ref, qr_ref, qbd_ref, wukt_ref, tt_ref, nlat_ref, nkr_ref, nk_ref, nv_ref = refs[1:10]
    lat_refs = refs[10:10 + n_pg]
    kr_refs = refs[10 + n_pg:10 + 2 * n_pg]
    o = 10 + 2 * n_pg
    k_refs = refs[o:o + n_head]
    v_refs = refs[o + n_head:o + 2 * n_head]
    o += 2 * n_head
    olat_ref, osb_ref, ocarry_ref = refs[o:o + 3]
    wst_scr, m_scr, l_scr, acc_scr = refs[o + 3:]

    j = pl.program_id(1)
    n_rows = MLA_HEADS * dec_seq
    n_sb = SB_HEADS * dec_seq
    kd = SB_KV_HEADS * SB_HEAD_DIM

    def mla_tile(c_pages, krt_pages, mask):
        cb = jnp.concatenate([p.astype(BF16) for p in c_pages], axis=0)
        a = _mm_nt(wst_scr[...], cb)
        n_keys = a.shape[1]
        yield
        krt = jnp.concatenate([p.astype(BF16) for p in krt_pages], axis=1)
        ssq = jnp.concatenate(
            [jnp.sum(jnp.square(a[h * MLA_NOPE:(h + 1) * MLA_NOPE]), axis=0, keepdims=True)
             for h in range(MLA_HEADS)], axis=0)
        r = lax.rsqrt(ssq * (1.0 / MLA_NOPE) + EPS)
        sn = a[MLA_HEADS * MLA_NOPE:]
        s = sn * jnp.concatenate([r] * dec_seq, axis=0) + _mm(qr_ref[...], krt)
        if mask is not None:
            s = jnp.where(mask(n_rows, n_keys, MLA_HEADS, True), s, NEG_INF)
        yield
        m_prev = m_scr[...]
        m_new = jnp.maximum(m_prev, jnp.max(s, axis=1, keepdims=True))
        alpha = jnp.exp(m_prev - m_new)
        p = jnp.exp(s - m_new[:, 0:1])
        l_scr[...] = alpha * l_scr[...] + jnp.sum(p, axis=1, keepdims=True)
        m_scr[...] = m_new
        yield
        acc_scr[...] = alpha * acc_scr[...] + _mm(p.astype(BF16), cb)

    def new_mask(n_r, n_k, div, inclusive):
        row = lax.broadcasted_iota(jnp.int32, (n_r, n_k), 0)
        col = lax.broadcasted_iota(jnp.int32, (n_r, n_k), 1)
        qpos = row // div if inclusive else row % div
        return (col <= qpos) if inclusive else (col < qpos)

    @pl.when(j == 0)
    def _():
        wst_scr[0:MLA_HEADS * MLA_NOPE, :] = wukt_ref[...]
        wst_scr[MLA_HEADS * MLA_NOPE:, :] = qabs_ref[...]
        m_scr[...] = jnp.full(m_scr.shape, NEG_INF, F32)
        l_scr[...] = jnp.zeros(l_scr.shape, F32)
        acc_scr[...] = jnp.zeros(acc_scr.shape, F32)
        tt = tt_ref[...]
        qbd = qbd_ref[...]
        g_new, r_new = _sample_sb_tile(qbd, tt, [nk_ref[...]], [nv_ref[...]],
                                       jnp.zeros((n_sb, LANES), F32), new_mask(n_sb, page, dec_seq, False), n_sb)
        _staggered([mla_tile([nlat_ref[...]], [nkr_ref[...]], new_mask), g_new], offset=0)
        g_head, r_head = _sample_sb_tile(qbd, tt, [r[...].reshape(kd, page) for r in k_refs],
                                         [r[...].reshape(kd, page) for r in v_refs], r_new[0], None, n_sb)
        for _ in g_head:
            pass
        ocarry_ref[...] = r_head[0]
        osb_ref[...] = r_new[1] + r_head[1]

    for _ in mla_tile([r[...] for r in lat_refs], [r[...] for r in kr_refs], None):
        pass

    @pl.when(j == pl.num_programs(1) - 1)
    def _():
        olat_ref[...] = acc_scr[...] / l_scr[...]


def _sample_tail_kernel(*refs, n_pg, dec_seq, page):
    pt_ref, alive_ref = refs[0:2]
    del pt_ref
    qbd_ref, tt_ref, acc_in_ref, carry_in_ref = refs[2:6]
    k_refs = refs[6:6 + n_pg]
    v_refs = refs[6 + n_pg:6 + 2 * n_pg]
    o_ref = refs[6 + 2 * n_pg]
    carry_scr, acc_scr = refs[7 + 2 * n_pg:]
    b = pl.program_id(0)
    j = pl.program_id(1)
    n_sb = SB_HEADS * dec_seq
    kd = SB_KV_HEADS * SB_HEAD_DIM

    @pl.when(j == 0)
    def _():
        carry_scr[...] = carry_in_ref[...]
        acc_scr[...] = acc_in_ref[...]

    @pl.when(alive_ref[b] > 0)
    def _():
        gen, res = _sample_sb_tile(qbd_ref[...], tt_ref[...], [r[...].reshape(kd, page) for r in k_refs],
                                   [r[...].reshape(kd, page) for r in v_refs], carry_scr[...], None, n_sb)
        for _ in gen:
            pass
        carry_scr[...] = res[0]
        acc_scr[...] += res[1]

    @pl.when(j == pl.num_programs(1) - 1)
    def _():
        o_ref[...] = acc_scr[...]


def _sample_call(layer, page_table, qabs, qr, qbd, wukt, tt, nlat, nkr, nk, nv,
                 cache_lat, cache_krt, cache_kt, cache_vt, n_pg, n_head):
    dec_batch, n_pages = page_table.shape
    page = cache_lat.shape[2]
    dec_seq = qabs.shape[1] // MLA_HEADS
    n_steps = n_pages // n_pg
    n_rows = MLA_HEADS * dec_seq
    n_sb = SB_HEADS * dec_seq
    kd = SB_KV_HEADS * SB_HEAD_DIM

    def per_seq(a):
        return pl.BlockSpec((None,) + a.shape[1:], lambda b, j, pt: (b,) + (0,) * (a.ndim - 1))

    def const(a):
        return pl.BlockSpec(a.shape, lambda b, j, pt: (0,) * a.ndim)

    def paged(a, g):
        nd = a.ndim - 2
        return pl.BlockSpec(
            (None, None) + a.shape[2:],
            lambda b, j, pt: (layer, pt[b, (n_steps - 1 - j) * n_pg + g]) + (0,) * nd)

    def head_page(a, g):
        nd = a.ndim - 2
        return pl.BlockSpec(
            (None, None) + a.shape[2:],
            lambda b, j, pt: (layer, pt[b, n_pages - n_head + g]) + (0,) * nd)

    in_specs = [per_seq(qabs), per_seq(qr), per_seq(qbd), const(wukt), const(tt),
                per_seq(nlat), per_seq(nkr), per_seq(nk), per_seq(nv)]
    operands = [qabs, qr, qbd, wukt, tt, nlat, nkr, nk, nv]
    for cache in (cache_lat, cache_krt):
        for g in range(n_pg):
            in_specs.append(paged(cache, g))
            operands.append(cache)
    for cache in (cache_kt, cache_vt):
        for g in range(n_head):
            in_specs.append(head_page(cache, g))
            operands.append(cache)

    return pl.pallas_call(
        functools.partial(_sample_kernel, n_pg=n_pg, n_head=n_head, dec_seq=dec_seq, page=page),
        grid_spec=pltpu.PrefetchScalarGridSpec(
            num_scalar_prefetch=1,
            grid=(dec_batch, n_steps),
            in_specs=in_specs,
            out_specs=(
                pl.BlockSpec((None, n_rows, KV_LORA), lambda b, j, pt: (b, 0, 0)),
                pl.BlockSpec((None, kd, n_sb), lambda b, j, pt: (b, 0, 0)),
                pl.BlockSpec((None, n_sb, LANES), lambda b, j, pt: (b, 0, 0)),
            ),
            scratch_shapes=[
                pltpu.VMEM((MLA_HEADS * MLA_NOPE + n_rows, KV_LORA), BF16),
                pltpu.VMEM((n_rows, LANES), F32),
                pltpu.VMEM((n_rows, LANES), F32),
                pltpu.VMEM((n_rows, KV_LORA), F32),
            ],
        ),
        out_shape=(
            jax.ShapeDtypeStruct((dec_batch, n_rows, KV_LORA), F32),
            jax.ShapeDtypeStruct((dec_batch, kd, n_sb), F32),
            jax.ShapeDtypeStruct((dec_batch, n_sb, LANES), F32),
        ),
        compiler_params=_cparams(("parallel", "arbitrary")),
        name="sample_attn",
    )(page_table, *operands)


def _sample_tail_call(layer, page_table, alive, qbd, tt, acc_in, carry_in, cache_kt, cache_vt, n_tail, n_pg):
    dec_batch = page_table.shape[0]
    page = cache_kt.shape[-1]
    n_sb = qbd.shape[1]
    dec_seq = n_sb // SB_HEADS
    n_steps = n_tail // n_pg
    kd = SB_KV_HEADS * SB_HEAD_DIM

    def per_seq(a):
        return pl.BlockSpec((None,) + a.shape[1:], lambda b, j, pt, al: (b,) + (0,) * (a.ndim - 1))

    def paged(a, g):
        nd = a.ndim - 2
        return pl.BlockSpec(
            (None, None) + a.shape[2:],
            lambda b, j, pt, al: (layer, jnp.where(al[b] > 0, pt[b, (n_steps - 1 - j) * n_pg + g], 0)) + (0,) * nd)

    in_specs = [per_seq(qbd), pl.BlockSpec(tt.shape, lambda b, j, pt, al: (0, 0)), per_seq(acc_in), per_seq(carry_in)]
    operands = [qbd, tt, acc_in, carry_in]
    for cache in (cache_kt, cache_vt):
        for g in range(n_pg):
            in_specs.append(paged(cache, g))
            operands.append(cache)

    return pl.pallas_call(
        functools.partial(_sample_tail_kernel, n_pg=n_pg, dec_seq=dec_seq, page=page),
        grid_spec=pltpu.PrefetchScalarGridSpec(
            num_scalar_prefetch=2,
            grid=(dec_batch, n_steps),
            in_specs=in_specs,
            out_specs=pl.BlockSpec((None, kd, n_sb), lambda b, j, pt, al: (b, 0, 0)),
            scratch_shapes=[pltpu.VMEM((n_sb, LANES), F32), pltpu.VMEM((kd, n_sb), F32)],
        ),
        out_shape=jax.ShapeDtypeStruct((dec_batch, kd, n_sb), F32),
        compiler_params=_cparams(("parallel", "arbitrary")),
        name="sample_sb_tail",
    )(page_table, alive, *operands)


def _uv_kernel(x_ref, w_ref, o_ref):
    o_ref[...] = _mm(x_ref[...].astype(BF16), w_ref[...]).astype(BF16)


def _uv_call(o_lat, wuv_bd):
    n = o_lat.shape[0]
    return pl.pallas_call(
        _uv_kernel,
        grid=(1,),
        in_specs=[pl.BlockSpec(o_lat.shape, lambda i: (0, 0)), pl.BlockSpec(wuv_bd.shape, lambda i: (0, 0))],
        out_specs=pl.BlockSpec((n, MLA_HEADS * MLA_V), lambda i: (0, 0)),
        out_shape=jax.ShapeDtypeStruct((n, MLA_HEADS * MLA_V), BF16),
        compiler_params=_cparams(("arbitrary",)),
        name="sample_uv",
    )(o_lat, wuv_bd)


def _merge_kernel(*refs, n_prompt_tiles, with_router):
    if with_router:
        (oap_ref, obp_ref, oas_ref, obs_ref, g_ref, h_ref, wa_ref, wb_ref, wo_ref, gf_ref, wr_ref,
         hn_ref, xn_ref, gates_ref) = refs
    else:
        (oap_ref, obp_ref, oas_ref, obs_ref, g_ref, h_ref, wa_ref, wb_ref, wo_ref, gf_ref,
         hn_ref, xn_ref) = refs
    is_prompt = pl.program_id(0) < n_prompt_tiles
    oa = jnp.where(is_prompt, oap_ref[...], oas_ref[...])
    ob = jnp.where(is_prompt, obp_ref[...], obs_ref[...])
    br_a = _mm(oa, wa_ref[...])
    br_b = _mm(ob, wb_ref[...])
    mixed = g_ref[:, 0:D_MODEL].astype(F32) * br_a + g_ref[:, D_MODEL:2 * D_MODEL].astype(F32) * br_b
    hn = h_ref[...] + _mm(mixed.astype(BF16), wo_ref[...])
    hn_ref[...] = hn
    xn = hn * lax.rsqrt(jnp.mean(hn * hn, axis=-1, keepdims=True) + EPS) * gf_ref[...]
    xn_ref[...] = xn.astype(BF16)
    if with_router:
        logits = jnp.dot(xn, wr_ref[...], preferred_element_type=F32, precision=lax.Precision.HIGHEST)
        lane = lax.broadcasted_iota(jnp.int32, logits.shape, 1)
        valid = lane < N_EXPERTS
        lg = jnp.where(valid, logits, NEG_INF)
        m1 = jnp.max(lg, axis=1, keepdims=True)
        i1 = jnp.min(jnp.where(lg == m1, lane, LANES), axis=1, keepdims=True)
        lg2 = jnp.where(lane == i1, NEG_INF, lg)
        m2 = jnp.max(lg2, axis=1, keepdims=True)
        i2 = jnp.min(jnp.where(lg2 == m2, lane, LANES), axis=1, keepdims=True)
        e2 = jnp.exp(m2 - m1)
        w1 = 1.0 / (1.0 + e2)
        w2 = e2 / (1.0 + e2)
        gates_ref[...] = jnp.where(lane == i1, w1, 0.0) + jnp.where(lane == i2, w2, 0.0)


def _merge_call(oa_p, ob_p, oa_s, ob_s, gates, h, wa, wb, wo, gf, wr, tm):
    T = h.shape[0]
    n_p = oa_p.shape[0] // tm
    with_router = wr is not None
    row = lambda w: pl.BlockSpec((tm, w), lambda i: (i, 0))
    full = lambda a: pl.BlockSpec(a.shape, lambda i: (0, 0))
    p_spec = pl.BlockSpec((tm, oa_p.shape[1]), lambda i: (jnp.minimum(i, n_p - 1), 0))
    s_spec = pl.BlockSpec((tm, oa_s.shape[1]), lambda i: (jnp.maximum(i - n_p, 0), 0))
    in_specs = [p_spec, p_spec, s_spec, s_spec, row(2 * D_MODEL), row(D_MODEL), full(wa), full(wb), full(wo), full(gf)]
    operands = [oa_p, ob_p, oa_s, ob_s, gates, h, wa, wb, wo, gf]
    out_shape = [jax.ShapeDtypeStruct((T, D_MODEL), F32), jax.ShapeDtypeStruct((T, D_MODEL), BF16)]
    out_specs = [row(D_MODEL), row(D_MODEL)]
    if with_router:
        in_specs.append(full(wr))
        operands.append(wr)
        out_shape.append(jax.ShapeDtypeStruct((T, LANES), F32))
        out_specs.append(row(LANES))
    return pl.pallas_call(
        functools.partial(_merge_kernel, n_prompt_tiles=n_p, with_router=with_router),
        grid=(T // tm,),
        in_specs=in_specs,
        out_specs=tuple(out_specs),
        out_shape=tuple(out_shape),
        compiler_params=_cparams(("parallel",)),
        name="merge",
    )(*operands)


def _swiglu_tile(x, wg, wu):
    a = _mm(x, wg)
    return (a * (1.0 / (1.0 + jnp.exp(-a))) * _mm(x, wu)).astype(BF16)


def _ffn_kernel(x_ref, h_ref, wg_ref, wu_ref, wd_ref, o_ref, acc_scr):
    f = pl.program_id(1)

    @pl.when(f == 0)
    def _():
        acc_scr[...] = h_ref[...]

    acc_scr[...] += _mm(_swiglu_tile(x_ref[...], wg_ref[...], wu_ref[...]), wd_ref[...])

    @pl.when(f == pl.num_programs(1) - 1)
    def _():
        o_ref[...] = acc_scr[...]


def _ffn_call(xn, h, wg, wu, wd, tm, tf):
    T = h.shape[0]
    d_ff = wg.shape[1]
    return pl.pallas_call(
        _ffn_kernel,
        grid=(T // tm, d_ff // tf),
        in_specs=[
            pl.BlockSpec((tm, D_MODEL), lambda i, f: (i, 0)),
            pl.BlockSpec((tm, D_MODEL), lambda i, f: (i, 0)),
            pl.BlockSpec((D_MODEL, tf), lambda i, f: (0, f)),
            pl.BlockSpec((D_MODEL, tf), lambda i, f: (0, f)),
            pl.BlockSpec((tf, D_MODEL), lambda i, f: (f, 0)),
        ],
        out_specs=pl.BlockSpec((tm, D_MODEL), lambda i, f: (i, 0)),
        out_shape=jax.ShapeDtypeStruct((T, D_MODEL), F32),
        scratch_shapes=[pltpu.VMEM((tm, D_MODEL), F32)],
        compiler_params=_cparams(("parallel", "arbitrary")),
        name="ffn_dense",
    )(xn, h, wg, wu, wd)


def _moe_kernel(x_ref, h_ref, gates_ref, wg_ref, wu_ref, wd_ref, o_ref, acc_scr, eacc_scr):
    e = pl.program_id(1)
    f = pl.program_id(2)
    last_f = f == pl.num_programs(2) - 1

    @pl.when((e == 0) & (f == 0))
    def _():
        acc_scr[...] = h_ref[...]

    y = _mm(_swiglu_tile(x_ref[...], wg_ref[...], wu_ref[...]), wd_ref[...])

    @pl.when(f == 0)
    def _():
        eacc_scr[...] = y

    @pl.when(f > 0)
    def _():
        eacc_scr[...] += y

    @pl.when(last_f)
    def _():
        lane = lax.broadcasted_iota(jnp.int32, gates_ref.shape, 1)
        gate = jnp.sum(jnp.where(lane == e, gates_ref[...], 0.0), axis=1, keepdims=True)
        acc_scr[...] += gate * eacc_scr[...]

    @pl.when(last_f & (e == pl.num_programs(1) - 1))
    def _():
        o_ref[...] = acc_scr[...]


def _moe_call(xn, h, gates, wg, wu, wd, tm, tf):
    T = h.shape[0]
    n_e, _, d_ff = wg.shape
    return pl.pallas_call(
        _moe_kernel,
        grid=(T // tm, n_e, d_ff // tf),
        in_specs=[
            pl.BlockSpec((tm, D_MODEL), lambda i, e, f: (i, 0)),
            pl.BlockSpec((tm, D_MODEL), lambda i, e, f: (i, 0)),
            pl.BlockSpec((tm, LANES), lambda i, e, f: (i, 0)),
            pl.BlockSpec((None, D_MODEL, tf), lambda i, e, f: (e, 0, f)),
            pl.BlockSpec((None, D_MODEL, tf), lambda i, e, f: (e, 0, f)),
            pl.BlockSpec((None, tf, D_MODEL), lambda i, e, f: (e, f, 0)),
        ],
        out_specs=pl.BlockSpec((tm, D_MODEL), lambda i, e, f: (i, 0)),
        out_shape=jax.ShapeDtypeStruct((T, D_MODEL), F32),
        scratch_shapes=[pltpu.VMEM((tm, D_MODEL), F32), pltpu.VMEM((tm, D_MODEL), F32)],
        compiler_params=_cparams(("parallel", "arbitrary", "arbitrary")),
        name="ffn_experts",
    )(xn, h, gates, wg, wu, wd)


def _head_block_cols(w_rope, w_nope):
    pad = jnp.zeros(w_rope.shape[:-1] + (LANES - MLA_ROPE - MLA_NOPE,), w_rope.dtype)
    blk = jnp.concatenate([w_rope, w_nope, pad], axis=-1)
    return blk.reshape(blk.shape[:-2] + (blk.shape[-2] * LANES,))


def _frame(rope=None, nope=None):
    v = jnp.zeros((LANES,), F32)
    if rope is not None:
        v = v.at[0:MLA_ROPE].set(rope)
    if nope is not None:
        v = v.at[NOPE_LO:NOPE_HI].set(nope)
    return v


def _row(v):
    return jnp.zeros((D_MODEL,), F32).at[0:v.shape[0]].set(v)


def _rope_table(pos):
    inv_freq = ROPE_THETA ** (-jnp.arange(HALF_ROPE, dtype=F32) / HALF_ROPE)
    ang = pos.astype(F32)[:, None] * inv_freq[None, :]
    cos, sin = jnp.cos(ang), jnp.sin(ang)
    n = pos.shape[0]
    z16 = jnp.zeros((n, HALF_ROPE), F32)
    cos_t = jnp.concatenate([cos, cos, jnp.ones((n, LANES - MLA_ROPE), F32)], axis=1)
    sina = jnp.concatenate([-sin, jnp.zeros((n, LANES - HALF_ROPE), F32)], axis=1)
    sinb = jnp.concatenate([z16, sin, jnp.zeros((n, LANES - MLA_ROPE), F32)], axis=1)
    return jnp.concatenate([cos_t, sina, sinb], axis=1)


def kernel(x_prompt, x_sample, cache_mla_latent, cache_mla_krope, cache_sb_k, cache_sb_v, page_table, g_attn_norm, w_in, g_q_lat, w_q_up, g_q_nope, g_q_rope, g_kv_lat, g_k_rope, w_uk, g_k_nope, w_uv, w_mla_out, w_sb_out, w_o, g_ffn_norm, w_ffn_gate, w_ffn_up, w_ffn_down, w_router, w_exp_gate, w_exp_up, w_exp_down):
    batch, seq, _ = x_prompt.shape
    dec_batch, dec_seq, _ = x_sample.shape
    depth = w_in.shape[0]
    n_pages = page_table.shape[1]
    page = cache_mla_latent.shape[2]
    past_len = n_pages * page
    t_p = batch * seq
    t_s = dec_batch * dec_seq
    T = t_p + t_s
    assert page == LANES and dec_seq <= page

    tm = _pick(math.gcd(t_p, t_s), (512, 256, 128, 64, 32, 16))
    tq = _pick(seq, (512, 256, 128))
    ck = min(SB_CHUNK, tq)
    n_pg = _pick(n_pages, (16, 8, 4, 2, 1))
    n_head = min(4, n_pages)
    n_tail = n_pages - n_head
    n_pg_tail = _pick(n_tail, (15, 12, 10, 8, 6, 5, 4, 3, 2, 1)) if n_tail else 0
    tm_ffn = _pick(T, (768, 512, 384, 256, 192, 128, 64, 32, 16))

    pos = jnp.concatenate([
        jnp.tile(jnp.arange(seq, dtype=jnp.int32), batch),
        jnp.tile(past_len + jnp.arange(dec_seq, dtype=jnp.int32), dec_batch)])
    tab = _rope_table(pos)
    tt_prompt = _suffix_matrix(ck)
    tt_page = _suffix_matrix(page)

    cache_krt = jnp.transpose(cache_mla_krope, (0, 1, 3, 2))
    cache_kt = jnp.transpose(cache_sb_k, (0, 1, 3, 4, 2))
    cache_vt = jnp.transpose(cache_sb_v, (0, 1, 3, 4, 2))

    h = jnp.concatenate([x_prompt.reshape(t_p, D_MODEL), x_sample.reshape(t_s, D_MODEL)], axis=0)
    outs = [[] for _ in range(8)]
    sizes = np.cumsum([Q_LORA, KV_LORA, MLA_ROPE, SB_HEADS * SB_HEAD_DIM, SB_KV_HEADS * SB_HEAD_DIM,
                       SB_KV_HEADS * SB_HEAD_DIM, D_MODEL])

    for l in range(depth):
        w_ql, w_c, w_kr, w_sq, w_sk, w_sv, w_ga, w_gb = jnp.split(w_in[l], sizes, axis=1)
        w_kr_pad = jnp.concatenate([w_kr, jnp.zeros((D_MODEL, LANES - MLA_ROPE), F32)], axis=1)
        w_all = jnp.concatenate([w_ql, w_c, w_kr_pad, w_sq * SB_SCALE, w_sk, w_sv, w_ga, w_gb], axis=1).astype(BF16)
        wq3 = w_q_up[l].reshape(Q_LORA, MLA_HEADS, MLA_NOPE + MLA_ROPE)
        wqu = _head_block_cols(wq3[..., MLA_NOPE:], wq3[..., :MLA_NOPE]).astype(BF16)
        wuk = _head_block_cols(jnp.zeros((KV_LORA, MLA_HEADS, MLA_ROPE), F32), w_uk[l]).astype(BF16)
        gv = jnp.stack([
            g_attn_norm[l], _row(g_q_lat[l]),
            _row(_frame(rope=g_q_rope[l] * MLA_SCALE)), _row(_frame(nope=g_q_nope[l] * MLA_SCALE)),
            _row(g_kv_lat[l]), _row(_frame(rope=g_k_rope[l])), _row(_frame(nope=g_k_nope[l])),
            jnp.zeros((D_MODEL,), F32)])
        wuv_bd = jnp.zeros((MLA_HEADS * KV_LORA, MLA_HEADS * MLA_V), F32)
        for hh in range(MLA_HEADS):
            wuv_bd = wuv_bd.at[hh * KV_LORA:(hh + 1) * KV_LORA, hh * MLA_V:(hh + 1) * MLA_V].set(w_uv[l][:, hh, :])
        wuv_bd = wuv_bd.astype(BF16)

        (qcat, kcat, c, cbf, kr, sq, sk, sv, skbf, svbf, gates) = _proj_call(h, tab, gv, w_all, wqu, wuk, tm)

        oa_p = _mla_call(qcat, kcat, cbf, wuv_bd, batch, seq, tq)
        k_heads = skbf[:t_p].reshape(batch, seq, SB_KV_HEADS, SB_HEAD_DIM).transpose(0, 2, 1, 3)
        v_heads = svbf[:t_p].reshape(batch, seq, SB_KV_HEADS, SB_HEAD_DIM).transpose(0, 2, 1, 3)
        v_heads = jnp.concatenate([v_heads, v_heads], axis=-1)
        ob_p = _sb_call(sq, k_heads, v_heads, tt_prompt, batch, seq, tq, ck)

        q_s = qcat[t_p:]
        wabs = jnp.zeros((MLA_HEADS, LANES, KV_LORA), F32)
        wabs = wabs.at[:, NOPE_LO:NOPE_HI, :].set(
            jnp.transpose(w_uk[l] * g_k_nope[l][None, None, :], (1, 2, 0)))
        qabs = _absorb_call(q_s, wabs.astype(BF16))
        qabs = qabs.reshape(dec_batch, dec_seq * MLA_HEADS, KV_LORA)
        qr = q_s.reshape(dec_batch, dec_seq * MLA_HEADS, LANES)[:, :, 0:MLA_ROPE]
        sq_s = sq[t_p:].reshape(dec_batch, dec_seq, SB_HEADS, SB_HEAD_DIM).transpose(0, 2, 1, 3)
        sq_s = sq_s.reshape(dec_batch, SB_KV_HEADS, SB_GROUP * dec_seq, SB_HEAD_DIM)
        eye = jnp.eye(SB_KV_HEADS, dtype=BF16)
        qbd = (sq_s[:, :, :, None, :] * eye[None, :, None, :, None]).reshape(
            dec_batch, SB_HEADS * dec_seq, SB_KV_HEADS * SB_HEAD_DIM)
        wukt = jnp.transpose(w_uk[l], (1, 2, 0)).reshape(MLA_HEADS * MLA_NOPE, KV_LORA).astype(BF16)
        pad_k = page - dec_seq
        nlat = jnp.pad(c[t_p:].reshape(dec_batch, dec_seq, KV_LORA), ((0, 0), (0, pad_k), (0, 0)))
        nkr = jnp.pad(kr[t_p:].reshape(dec_batch, dec_seq, MLA_ROPE).transpose(0, 2, 1), ((0, 0), (0, 0), (0, pad_k)))
        nk = jnp.pad(sk[t_p:].reshape(dec_batch, dec_seq, -1).transpose(0, 2, 1), ((0, 0), (0, 0), (0, pad_k)))
        nv = jnp.pad(sv[t_p:].reshape(dec_batch, dec_seq, -1).transpose(0, 2, 1), ((0, 0), (0, 0), (0, pad_k)))
        o_lat_s, o_sbt, sb_carry = _sample_call(l, page_table, qabs, qr, qbd, wukt, tt_page, nlat, nkr, nk, nv,
                                                cache_mla_latent, cache_krt, cache_kt, cache_vt, n_pg, n_head)
        if n_tail > 0:
            alive = (jnp.max(sb_carry, axis=(1, 2)) >= SB_DEAD_LOG).astype(jnp.int32)
            o_sbt = lax.cond(
                jnp.max(alive) > 0,
                functools.partial(_sample_tail_call, l, page_table, alive, qbd, tt_page, o_sbt, sb_carry,
                                  cache_kt, cache_vt, n_tail, n_pg_tail),
                lambda: o_sbt)
        o_lat_s = o_lat_s.reshape(dec_batch, dec_seq, MLA_HEADS, KV_LORA).reshape(t_s, MLA_HEADS * KV_LORA)
        oa_s = _uv_call(o_lat_s, wuv_bd)
        o5 = o_sbt.reshape(dec_batch, SB_KV_HEADS, SB_HEAD_DIM, SB_KV_HEADS, SB_GROUP, dec_seq)
        o5 = jnp.stack([o5[:, kv, :, kv] for kv in range(SB_KV_HEADS)], axis=1)
        ob_s = o5.transpose(0, 4, 1, 3, 2).reshape(t_s, SB_HEADS * SB_HEAD_DIM).astype(BF16)

        j = l // 2
        moe = l % 2 == 1
        wr = None
        if moe:
            wr = jnp.concatenate([w_router[j], jnp.zeros((D_MODEL, LANES - N_EXPERTS), F32)], axis=1)
        res = _merge_call(oa_p, ob_p, oa_s, ob_s, gates, h,
                          w_mla_out[l].astype(BF16), w_sb_out[l].astype(BF16), w_o[l].astype(BF16),
                          g_ffn_norm[l][None, :], wr, tm)
        if moe:
            hn, xn, route = res
            h = _moe_call(xn, hn, route, w_exp_gate[j].astype(BF16), w_exp_up[j].astype(BF16),
                          w_exp_down[j].astype(BF16), tm_ffn, 512)
        else:
            hn, xn = res
            d_ff = w_ffn_gate.shape[2]
            h = _ffn_call(xn, hn, w_ffn_gate[j].astype(BF16), w_ffn_up[j].astype(BF16),
                          w_ffn_down[j].astype(BF16), tm_ffn, _pick(d_ff, (1408, 512, 256, 128)))

        for idx, a in enumerate((c, kr, sk, sv)):
            outs[idx].append(a[:t_p])
            outs[4 + idx].append(a[t_p:])

    def stack_p(xs, tail):
        return jnp.stack(xs).reshape((depth, batch, seq) + tail)

    def stack_s(xs, tail):
        return jnp.stack(xs).reshape((depth, dec_batch, dec_seq) + tail)

    kv_tail = (SB_KV_HEADS, SB_HEAD_DIM)
    return (h[:t_p].reshape(batch, seq, D_MODEL), h[t_p:].reshape(dec_batch, dec_seq, D_MODEL),
            stack_p(outs[0], (KV_LORA,)), stack_p(outs[1], (MLA_ROPE,)),
            stack_p(outs[2], kv_tail), stack_p(outs[3], kv_tail),
            stack_s(outs[4], (KV_LORA,)), stack_s(outs[5], (MLA_ROPE,)),
            stack_s(outs[6], kv_tail), stack_s(outs[7], kv_tail))
```

```python
import functools
import math

import jax
import jax.numpy as jnp
import numpy as np
from jax import lax
from jax.experimental import pallas as pl
from jax.experimental.pallas import tpu as pltpu

D_MODEL = 1024
MLA_HEADS = 8
MLA_NOPE = 64
MLA_ROPE = 32
MLA_V = 64
Q_LORA = 256
KV_LORA = 128
ROPE_THETA = 10000.0
MLA_SCALE = (MLA_NOPE + MLA_ROPE) ** -0.5
SB_HEADS = 8
SB_KV_HEADS = 4
SB_GROUP = SB_HEADS // SB_KV_HEADS
SB_HEAD_DIM = 64
SB_SCALE = SB_HEAD_DIM ** -0.5
N_EXPERTS = 8
EPS = 1e-6
NEG_INF = -1e30

LANES = 128
HALF_ROPE = MLA_ROPE // 2
NOPE_LO, NOPE_HI = MLA_ROPE, MLA_ROPE + MLA_NOPE
VMEM_LIMIT = 56 * 1024 * 1024

F32 = jnp.float32
BF16 = jnp.bfloat16


def _pick(n, candidates):
    for c in candidates:
        if n % c == 0:
            return c
    return n


def _mm(a, b):
    return jnp.dot(a, b, preferred_element_type=F32)


def _mm_nt(a, b):
    return lax.dot_general(a, b, (((1,), (1,)), ((), ())), preferred_element_type=F32)


def _cparams(sem):
    return pltpu.CompilerParams(dimension_semantics=sem, vmem_limit_bytes=VMEM_LIMIT)


def _staggered(gens, offset=1):
    pending = list(gens)
    active = []
    rnd = 0
    while pending or active:
        if pending and (offset == 0 or rnd % offset == 0):
            if offset == 0:
                active.extend(pending)
                pending = []
            else:
                active.append(pending.pop(0))
        for g in list(active):
            try:
                next(g)
            except StopIteration:
                active.remove(g)
        rnd += 1


_C_QLAT, _C_CKV, _C_KR, _C_SQ, _C_SK, _C_SV, _C_G, _C_END = 0, 256, 384, 512, 1024, 1280, 1536, 3584


def _rope_rotate(y, cos, sina, sinb):
    return y * cos + pltpu.roll(y, LANES - HALF_ROPE, 1) * sina + pltpu.roll(y, HALF_ROPE, 1) * sinb


def _proj_kernel(h_ref, tab_ref, gv_ref, w_ref, wqu_ref, wuk_ref,
                 qcat_ref, kcat_ref, c_ref, cbf_ref, kr_ref, sq_ref, sk_ref, sv_ref,
                 skbf_ref, svbf_ref, gates_ref):
    x = h_ref[...]
    g_attn = gv_ref[0:1, :]
    xn = (x * lax.rsqrt(jnp.mean(x * x, axis=-1, keepdims=True) + EPS) * g_attn).astype(BF16)

    cos = tab_ref[:, 0:LANES]
    sina = tab_ref[:, LANES:2 * LANES]
    sinb = tab_ref[:, 2 * LANES:3 * LANES]
    lane = lax.broadcasted_iota(jnp.int32, (1, LANES), 1)
    m_rope = (lane < MLA_ROPE).astype(F32)
    m_nope = ((lane >= NOPE_LO) & (lane < NOPE_HI)).astype(F32)

    q_lat = _mm(xn, w_ref[:, _C_QLAT:_C_CKV])
    g_ql = gv_ref[1:2, 0:Q_LORA]
    ql = (q_lat * lax.rsqrt(jnp.mean(q_lat * q_lat, axis=-1, keepdims=True) + EPS) * g_ql).astype(BF16)
    q = _mm(ql, wqu_ref[...])
    g_qr = gv_ref[2:3, 0:LANES]
    g_qn = gv_ref[3:4, 0:LANES]
    for h in range(MLA_HEADS):
        xh = q[:, h * LANES:(h + 1) * LANES]
        x2 = xh * xh
        ss_r = jnp.sum(x2 * m_rope, axis=-1, keepdims=True)
        ss_n = jnp.sum(x2 * m_nope, axis=-1, keepdims=True)
        sc = lax.rsqrt(ss_r * (1.0 / MLA_ROPE) + EPS) * g_qr + lax.rsqrt(ss_n * (1.0 / MLA_NOPE) + EPS) * g_qn
        y = _rope_rotate(xh * sc, cos, sina, sinb)
        qcat_ref[:, h * LANES:(h + 1) * LANES] = y.astype(BF16)

    c_raw = _mm(xn, w_ref[:, _C_CKV:_C_KR])
    g_kvl = gv_ref[4:5, 0:KV_LORA]
    c = c_raw * lax.rsqrt(jnp.mean(c_raw * c_raw, axis=-1, keepdims=True) + EPS) * g_kvl
    c_ref[...] = c
    cbf = c.astype(BF16)
    cbf_ref[...] = cbf

    kr_raw = _mm(xn, w_ref[:, _C_KR:_C_SQ])
    g_kr = gv_ref[5:6, 0:LANES]
    ss = jnp.sum(kr_raw * kr_raw, axis=-1, keepdims=True)
    krn = _rope_rotate(kr_raw * (lax.rsqrt(ss * (1.0 / MLA_ROPE) + EPS) * g_kr), cos, sina, sinb)
    kr_ref[...] = krn[:, 0:MLA_ROPE]

    kp = _mm(cbf, wuk_ref[...])
    g_kn = gv_ref[6:7, 0:LANES]
    for h in range(MLA_HEADS):
        kh = kp[:, h * LANES:(h + 1) * LANES]
        ss_n = jnp.sum(kh * kh, axis=-1, keepdims=True)
        kn = kh * (lax.rsqrt(ss_n * (1.0 / MLA_NOPE) + EPS) * g_kn)
        kcat_ref[:, h * LANES:(h + 1) * LANES] = (kn + krn).astype(BF16)

    sq_ref[...] = _mm(xn, w_ref[:, _C_SQ:_C_SK]).astype(BF16)
    sk = _mm(xn, w_ref[:, _C_SK:_C_SV])
    sk_ref[...] = sk
    skbf_ref[...] = sk.astype(BF16)
    sv = _mm(xn, w_ref[:, _C_SV:_C_G])
    sv_ref[...] = sv
    svbf_ref[...] = sv.astype(BF16)

    g = _mm(xn, w_ref[:, _C_G:_C_END])
    gates_ref[...] = (1.0 / (1.0 + jnp.exp(-g))).astype(BF16)


def _proj_call(h, tab, gv, w_all, wqu, wuk, tm):
    T = h.shape[0]
    row = lambda w: pl.BlockSpec((tm, w), lambda i: (i, 0))
    full = lambda a: pl.BlockSpec(a.shape, lambda i: (0, 0))
    out_shapes = (
        jax.ShapeDtypeStruct((T, MLA_HEADS * LANES), BF16),
        jax.ShapeDtypeStruct((T, MLA_HEADS * LANES), BF16),
        jax.ShapeDtypeStruct((T, KV_LORA), F32),
        jax.ShapeDtypeStruct((T, KV_LORA), BF16),
        jax.ShapeDtypeStruct((T, MLA_ROPE), F32),
        jax.ShapeDtypeStruct((T, SB_HEADS * SB_HEAD_DIM), BF16),
        jax.ShapeDtypeStruct((T, SB_KV_HEADS * SB_HEAD_DIM), F32),
        jax.ShapeDtypeStruct((T, SB_KV_HEADS * SB_HEAD_DIM), F32),
        jax.ShapeDtypeStruct((T, SB_KV_HEADS * SB_HEAD_DIM), BF16),
        jax.ShapeDtypeStruct((T, SB_KV_HEADS * SB_HEAD_DIM), BF16),
        jax.ShapeDtypeStruct((T, 2 * D_MODEL), BF16),
    )
    return pl.pallas_call(
        _proj_kernel,
        grid=(T // tm,),
        in_specs=[row(D_MODEL), row(3 * LANES), full(gv), full(w_all), full(wqu), full(wuk)],
        out_specs=tuple(row(s.shape[1]) for s in out_shapes),
        out_shape=out_shapes,
        compiler_params=_cparams(("parallel",)),
        name="proj",
    )(h, tab, gv, w_all, wqu, wuk)


def _mla_kernel(qi_tab, ki_tab, q_ref, k_ref, v_ref, wuv_ref, o_ref, m_scr, l_scr, acc_scr, *, tq):
    t = pl.program_id(1)
    qi = qi_tab[t]
    ki = ki_tab[t]

    @pl.when(ki == 0)
    def _():
        m_scr[...] = jnp.full(m_scr.shape, NEG_INF, F32)
        l_scr[...] = jnp.zeros(l_scr.shape, F32)
        acc_scr[...] = jnp.zeros(acc_scr.shape, F32)

    def step(masked):
        v = v_ref[...]
        if masked:
            row = lax.broadcasted_iota(jnp.int32, (tq, tq), 0)
            col = lax.broadcasted_iota(jnp.int32, (tq, tq), 1)
            keep = col <= row

        for h in range(MLA_HEADS):
            s = _mm_nt(q_ref[:, h * LANES:(h + 1) * LANES], k_ref[:, h * LANES:(h + 1) * LANES])
            if masked:
                s = jnp.where(keep, s, NEG_INF)
            m_prev = m_scr[h]
            m_new = jnp.maximum(m_prev, jnp.max(s, axis=1, keepdims=True))
            alpha = jnp.exp(m_prev - m_new)
            p = jnp.exp(s - m_new[:, 0:1])
            l_scr[h] = alpha * l_scr[h] + jnp.sum(p, axis=1, keepdims=True)
            acc_scr[h] = alpha * acc_scr[h] + _mm(p.astype(BF16), v)
            m_scr[h] = m_new

    @pl.when(ki < qi)
    def _():
        step(False)

    @pl.when(ki == qi)
    def _():
        step(True)
        o_lat = jnp.concatenate([(acc_scr[h] / l_scr[h]).astype(BF16) for h in range(MLA_HEADS)], axis=1)
        o_ref[...] = _mm(o_lat, wuv_ref[...]).astype(BF16)


def _tri_tables(nq, descending):
    qi, ki = [], []
    for i in range(nq):
        ks = range(i, -1, -1) if descending else range(i + 1)
        for j in ks:
            qi.append(i)
            ki.append(j)
    return jnp.asarray(np.array(qi, np.int32)), jnp.asarray(np.array(ki, np.int32))


def _mla_call(qcat, kcat, cbf, wuv_bd, batch, seq, tq):
    nq = seq // tq
    qi_tab, ki_tab = _tri_tables(nq, descending=False)
    n_tri = int(qi_tab.shape[0])
    return pl.pallas_call(
        functools.partial(_mla_kernel, tq=tq),
        grid_spec=pltpu.PrefetchScalarGridSpec(
            num_scalar_prefetch=2,
            grid=(batch, n_tri),
            in_specs=[
                pl.BlockSpec((tq, MLA_HEADS * LANES), lambda b, t, qt, kt: (b * nq + qt[t], 0)),
                pl.BlockSpec((tq, MLA_HEADS * LANES), lambda b, t, qt, kt: (b * nq + kt[t], 0)),
                pl.BlockSpec((tq, KV_LORA), lambda b, t, qt, kt: (b * nq + kt[t], 0)),
                pl.BlockSpec(wuv_bd.shape, lambda b, t, qt, kt: (0, 0)),
            ],
            out_specs=pl.BlockSpec((tq, MLA_HEADS * MLA_V), lambda b, t, qt, kt: (b * nq + qt[t], 0)),
            scratch_shapes=[
                pltpu.VMEM((MLA_HEADS, tq, LANES), F32),
                pltpu.VMEM((MLA_HEADS, tq, LANES), F32),
                pltpu.VMEM((MLA_HEADS, tq, KV_LORA), F32),
            ],
        ),
        out_shape=jax.ShapeDtypeStruct((batch * seq, MLA_HEADS * MLA_V), BF16),
        compiler_params=_cparams(("parallel", "arbitrary")),
        name="mla_prompt",
    )(qi_tab, ki_tab, qcat, kcat, cbf, wuv_bd)


SB_CHUNK = 256


def _suffix_matrix(n):
    j = np.arange(n)[:, None]
    s = np.arange(n)[None, :]
    t = (j > s).astype(np.float32)
    return jnp.asarray(np.concatenate([t, t], axis=0), dtype=BF16)


def _sb_terms(z):
    lp = jnp.log(1.0 + jnp.exp(-jnp.abs(z)))
    log_beta = jnp.minimum(z, 0.0) - lp
    return log_beta, log_beta - z


def _split_bf16(x):
    hi = x.astype(BF16)
    return hi, (x - hi.astype(F32)).astype(BF16)


SB_DEAD_LOG = -104.0


def _sb_chunk(q2, k, v, tt, carry, keep, tq):
    res = []

    def gen():
        z = _mm_nt(q2, k)
        yield
        log_beta, l1 = _sb_terms(z)
        if keep is not None:
            l1 = jnp.where(keep, l1, 0.0)
        hi, lo = _split_bf16(l1)
        res.append(carry + jnp.sum(l1, axis=1, keepdims=True))
        yield
        after = _mm(jnp.concatenate([hi, lo], axis=1), tt)
        yield
        w = jnp.exp(log_beta + after + carry[:, 0:1])
        if keep is not None:
            w = jnp.where(keep, w, 0.0)
        w = w.astype(BF16)
        yield
        o2 = _mm(w, v)
        first_head = lax.broadcasted_iota(jnp.int32, (1, LANES), 1) < SB_HEAD_DIM
        res.append(jnp.where(first_head, o2[0:tq], o2[tq:2 * tq]))

    return gen(), res


def _sb_kernel(q_ref, k_ref, v_ref, tt_ref, o_ref, carry_scr, acc_scr, *, tq, ck):
    qi = pl.program_id(2)
    n_chunks = tq // ck
    tt = tt_ref[...]
    q2 = jnp.concatenate([q_ref[:, g * SB_HEAD_DIM:(g + 1) * SB_HEAD_DIM] for g in range(SB_GROUP)], axis=0)

    base = pl.multiple_of(qi * tq, tq)
    row = lax.broadcasted_iota(jnp.int32, (SB_GROUP * tq, ck), 0) & (tq - 1)
    col0 = lax.broadcasted_iota(jnp.int32, (SB_GROUP * tq, ck), 1)
    carry = jnp.zeros((SB_GROUP * tq, LANES), F32)
    gens, results = [], []
    carry_box = [carry]

    def diag_chunk(c):
        k = k_ref[pl.ds(base + c * ck, ck), :]
        v = v_ref[pl.ds(base + c * ck, ck), :]
        z = _mm_nt(q2, k)
        yield
        keep = (col0 + c * ck) < row
        log_beta, l1 = _sb_terms(z)
        l1 = jnp.where(keep, l1, 0.0)
        hi, lo = _split_bf16(l1)
        carry_in = carry_box[-1]
        carry_box.append(carry_in + jnp.sum(l1, axis=1, keepdims=True))
        yield
        after = _mm(jnp.concatenate([hi, lo], axis=1), tt)
        yield
        w = jnp.where(keep, jnp.exp(log_beta + after + carry_in[:, 0:1]), 0.0).astype(BF16)
        yield
        o2 = _mm(w, v)
        first_head = lax.broadcasted_iota(jnp.int32, (1, LANES), 1) < SB_HEAD_DIM
        results.append(jnp.where(first_head, o2[0:tq], o2[tq:2 * tq]))

    _staggered([diag_chunk(c) for c in range(n_chunks - 1, -1, -1)])
    carry_scr[...] = carry_box[-1]
    acc_scr[...] = functools.reduce(lambda a, b: a + b, results)

    def alive_of(c):
        return (jnp.max(c) >= SB_DEAD_LOG).astype(jnp.int32)

    def cond(st):
        j, alive = st
        return (j >= 0) & (alive > 0)

    def body(st):
        j, _ = st
        start = pl.multiple_of(j * ck, ck)
        gen, res = _sb_chunk(q2, k_ref[pl.ds(start, ck), :], v_ref[pl.ds(start, ck), :], tt,
                             carry_scr[...], None, tq)
        for _ in gen:
            pass
        carry_new, out = res
        carry_scr[...] = carry_new
        acc_scr[...] += out
        return j - 1, alive_of(carry_new)

    lax.while_loop(cond, body, (qi * n_chunks - 1, alive_of(carry_box[-1])))
    o_ref[...] = acc_scr[...].astype(BF16)


def _sb_call(sq, k_heads, v_heads, tt, batch, seq, tq, ck):
    nq = seq // tq
    gw = SB_GROUP * SB_HEAD_DIM
    return pl.pallas_call(
        functools.partial(_sb_kernel, tq=tq, ck=ck),
        grid=(batch, SB_KV_HEADS, nq),
        in_specs=[
            pl.BlockSpec((tq, gw), lambda b, kv, i: (b * nq + i, kv)),
            pl.BlockSpec((None, None, seq, SB_HEAD_DIM), lambda b, kv, i: (b, kv, 0, 0)),
            pl.BlockSpec((None, None, seq, gw), lambda b, kv, i: (b, kv, 0, 0)),
            pl.BlockSpec(tt.shape, lambda b, kv, i: (0, 0)),
        ],
        out_specs=pl.BlockSpec((tq, gw), lambda b, kv, i: (b * nq + i, kv)),
        scratch_shapes=[
            pltpu.VMEM((SB_GROUP * tq, LANES), F32),
            pltpu.VMEM((tq, gw), F32),
        ],
        out_shape=jax.ShapeDtypeStruct((batch * seq, SB_HEADS * SB_HEAD_DIM), BF16),
        compiler_params=_cparams(("parallel", "parallel", "arbitrary")),
        name="sb_prompt",
    )(sq, k_heads, v_heads, tt)


def _absorb_kernel(q_ref, w_ref, o_ref):
    for h in range(MLA_HEADS):
        o_ref[:, h * LANES:(h + 1) * LANES] = _mm(q_ref[:, h * LANES:(h + 1) * LANES], w_ref[h]).astype(BF16)


def _absorb_call(q_s, wabs):
    n = q_s.shape[0]
    return pl.pallas_call(
        _absorb_kernel,
        grid=(1,),
        in_specs=[pl.BlockSpec(q_s.shape, lambda i: (0, 0)), pl.BlockSpec(wabs.shape, lambda i: (0, 0, 0))],
        out_specs=pl.BlockSpec((n, MLA_HEADS * LANES), lambda i: (0, 0)),
        out_shape=jax.ShapeDtypeStruct((n, MLA_HEADS * LANES), BF16),
        compiler_params=_cparams(("arbitrary",)),
        name="absorb_q",
    )(q_s, wabs)


def _sample_sb_tile(qbd, tt, kt_pages, vt_pages, carry_in, keep, n_sb):
    res = []

    def gen():
        n = len(kt_pages)
        z = jnp.concatenate([_mm(qbd, p.astype(BF16)) for p in kt_pages], axis=0)
        yield
        log_beta, l1 = _sb_terms(z)
        if keep is not None:
            l1 = jnp.where(keep, l1, 0.0)
        hi, lo = _split_bf16(l1)
        tot = jnp.sum(l1, axis=1, keepdims=True)
        run = carry_in
        carries = [None] * n
        for i in range(n - 1, -1, -1):
            carries[i] = run
            run = run + tot[i * n_sb:(i + 1) * n_sb]
        res.append(run)
        carry = jnp.concatenate([c[:, 0:1] for c in carries], axis=0)
        yield
        after = _mm(jnp.concatenate([hi, lo], axis=1), tt)
        yield
        w = jnp.exp(log_beta + after + carry)
        if keep is not None:
            w = jnp.where(keep, w, 0.0)
        wl = jnp.concatenate([w[i * n_sb:(i + 1) * n_sb].astype(BF16) for i in range(n)], axis=1)
        vt = jnp.concatenate([p.astype(BF16) for p in vt_pages], axis=1)
        res.append(_mm_nt(vt, wl))

    return gen(), res


def _sample_kernel(*refs, n_pg, n_head, dec_seq, page):
    pt_ref = refs[0]
    del pt_ref
    qabs_ref, qr_ref, qbd_ref, wukt_ref, tt_ref, nlat_ref, nkr_ref, nk_ref, nv_ref = refs[1:10]
    lat_refs = refs[10:10 + n_pg]
    kr_refs = refs[10 + n_pg:10 + 2 * n_pg]
    o = 10 + 2 * n_pg
    k_refs = refs[o:o + n_head]
    v_refs = refs[o + n_head:o + 2 * n_head]
    o += 2 * n_head
    olat_ref, osb_ref, ocarry_ref = refs[o:o + 3]
    wst_scr, m_scr, l_scr, acc_scr = refs[o + 3:]

    j = pl.program_id(1)
    n_rows = MLA_HEADS * dec_seq
    n_sb = SB_HEADS * dec_seq
    kd = SB_KV_HEADS * SB_HEAD_DIM

    def mla_tile(c_pages, krt_pages, mask):
        cb = jnp.concatenate([p.astype(BF16) for p in c_pages], axis=0)
        a = _mm_nt(wst_scr[...], cb)
        n_keys = a.shape[1]
        yield
        krt = jnp.concatenate([p.astype(BF16) for p in krt_pages], axis=1)
        ssq = jnp.concatenate(
            [jnp.sum(jnp.square(a[h * MLA_NOPE:(h + 1) * MLA_NOPE]), axis=0, keepdims=True)
             for h in range(MLA_HEADS)], axis=0)
        r = lax.rsqrt(ssq * (1.0 / MLA_NOPE) + EPS)
        sn = a[MLA_HEADS * MLA_NOPE:]
        s = sn * jnp.concatenate([r] * dec_seq, axis=0) + _mm(qr_ref[...], krt)
        if mask is not None:
            s = jnp.where(mask(n_rows, n_keys, MLA_HEADS, True), s, NEG_INF)
        yield
        m_prev = m_scr[...]
        m_new = jnp.maximum(m_prev, jnp.max(s, axis=1, keepdims=True))
        alpha = jnp.exp(m_prev - m_new)
        p = jnp.exp(s - m_new[:, 0:1])
        l_scr[...] = alpha * l_scr[...] + jnp.sum(p, axis=1, keepdims=True)
        m_scr[...] = m_new
        yield
        acc_scr[...] = alpha * acc_scr[...] + _mm(p.astype(BF16), cb)

    def new_mask(n_r, n_k, div, inclusive):
        row = lax.broadcasted_iota(jnp.int32, (n_r, n_k), 0)
        col = lax.broadcasted_iota(jnp.int32, (n_r, n_k), 1)
        qpos = row // div if inclusive else row % div
        return (col <= qpos) if inclusive else (col < qpos)

    @pl.when(j == 0)
    def _():
        wst_scr[0:MLA_HEADS * MLA_NOPE, :] = wukt_ref[...]
        wst_scr[MLA_HEADS * MLA_NOPE:, :] = qabs_ref[...]
        m_scr[...] = jnp.full(m_scr.shape, NEG_INF, F32)
        l_scr[...] = jnp.zeros(l_scr.shape, F32)
        acc_scr[...] = jnp.zeros(acc_scr.shape, F32)
        tt = tt_ref[...]
        qbd = qbd_ref[...]
        g_new, r_new = _sample_sb_tile(qbd, tt, [nk_ref[...]], [nv_ref[...]],
                                       jnp.zeros((n_sb, LANES), F32), new_mask(n_sb, page, dec_seq, False), n_sb)
        _staggered([mla_tile([nlat_ref[...]], [nkr_ref[...]], new_mask), g_new], offset=0)
        g_head, r_head = _sample_sb_tile(qbd, tt, [r[...].reshape(kd, page) for r in k_refs],
                                         [r[...].reshape(kd, page) for r in v_refs], r_new[0], None, n_sb)
        for _ in g_head:
            pass
        ocarry_ref[...] = r_head[0]
        osb_ref[...] = r_new[1] + r_head[1]

    for _ in mla_tile([r[...] for r in lat_refs], [r[...] for r in kr_refs], None):
        pass

    @pl.when(j == pl.num_programs(1) - 1)
    def _():
        olat_ref[...] = acc_scr[...] / l_scr[...]


def _sample_tail_kernel(*refs, n_pg, dec_seq, page):
    pt_ref, alive_ref = refs[0:2]
    del pt_ref
    qbd_ref, tt_ref, acc_in_ref, carry_in_ref = refs[2:6]
    k_refs = refs[6:6 + n_pg]
    v_refs = refs[6 + n_pg:6 + 2 * n_pg]
    o_ref = refs[6 + 2 * n_pg]
    carry_scr, acc_scr = refs[7 + 2 * n_pg:]
    b = pl.program_id(0)
    j = pl.program_id(1)
    n_sb = SB_HEADS * dec_seq
    kd = SB_KV_HEADS * SB_HEAD_DIM

    @pl.when(j == 0)
    def _():
        carry_scr[...] = carry_in_ref[...]
        acc_scr[...] = acc_in_ref[...]

    @pl.when(alive_ref[b] > 0)
    def _():
        gen, res = _sample_sb_tile(qbd_ref[...], tt_ref[...], [r[...].reshape(kd, page) for r in k_refs],
                                   [r[...].reshape(kd, page) for r in v_refs], carry_scr[...], None, n_sb)
        for _ in gen:
            pass
        carry_scr[...] = res[0]
        acc_scr[...] += res[1]

    @pl.when(j == pl.num_programs(1) - 1)
    def _():
        o_ref[...] = acc_scr[...]


def _sample_call(layer, page_table, qabs, qr, qbd, wukt, tt, nlat, nkr, nk, nv,
                 cache_lat, cache_krt, cache_kt, cache_vt, n_pg, n_head):
    dec_batch, n_pages = page_table.shape
    page = cache_lat.shape[2]
    dec_seq = qabs.shape[1] // MLA_HEADS
    n_steps = n_pages // n_pg
    n_rows = MLA_HEADS * dec_seq
    n_sb = SB_HEADS * dec_seq
    kd = SB_KV_HEADS * SB_HEAD_DIM

    def per_seq(a):
        return pl.BlockSpec((None,) + a.shape[1:], lambda b, j, pt: (b,) + (0,) * (a.ndim - 1))

    def const(a):
        return pl.BlockSpec(a.shape, lambda b, j, pt: (0,) * a.ndim)

    def paged(a, g):
        nd = a.ndim - 2
        return pl.BlockSpec(
            (None, None) + a.shape[2:],
            lambda b, j, pt: (layer, pt[b, (n_steps - 1 - j) * n_pg + g]) + (0,) * nd)

    def head_page(a, g):
        nd = a.ndim - 2
        return pl.BlockSpec(
            (None, None) + a.shape[2:],
            lambda b, j, pt: (layer, pt[b, n_pages - n_head + g]) + (0,) * nd)

    in_specs = [per_seq(qabs), per_seq(qr), per_seq(qbd), const(wukt), const(tt),
                per_seq(nlat), per_seq(nkr), per_seq(nk), per_seq(nv)]
    operands = [qabs, qr, qbd, wukt, tt, nlat, nkr, nk, nv]
    for cache in (cache_lat, cache_krt):
        for g in range(n_pg):
            in_specs.append(paged(cache, g))
            operands.append(cache)
    for cache in (cache_kt, cache_vt):
        for g in range(n_head):
            in_specs.append(head_page(cache, g))
            operands.append(cache)

    return pl.pallas_call(
        functools.partial(_sample_kernel, n_pg=n_pg, n_head=n_head, dec_seq=dec_seq, page=page),
        grid_spec=pltpu.PrefetchScalarGridSpec(
            num_scalar_prefetch=1,
            grid=(dec_batch, n_steps),
            in_specs=in_specs,
            out_specs=(
                pl.BlockSpec((None, n_rows, KV_LORA), lambda b, j, pt: (b, 0, 0)),
                pl.BlockSpec((None, kd, n_sb), lambda b, j, pt: (b, 0, 0)),
                pl.BlockSpec((None, n_sb, LANES), lambda b, j, pt: (b, 0, 0)),
            ),
            scratch_shapes=[
                pltpu.VMEM((MLA_HEADS * MLA_NOPE + n_rows, KV_LORA), BF16),
                pltpu.VMEM((n_rows, LANES), F32),
                pltpu.VMEM((n_rows, LANES), F32),
                pltpu.VMEM((n_rows, KV_LORA), F32),
            ],
        ),
        out_shape=(
            jax.ShapeDtypeStruct((dec_batch, n_rows, KV_LORA), F32),
            jax.ShapeDtypeStruct((dec_batch, kd, n_sb), F32),
            jax.ShapeDtypeStruct((dec_batch, n_sb, LANES), F32),
        ),
        compiler_params=_cparams(("parallel", "arbitrary")),
        name="sample_attn",
    )(page_table, *operands)


def _sample_tail_call(layer, page_table, alive, qbd, tt, acc_in, carry_in, cache_kt, cache_vt, n_tail, n_pg):
    dec_batch = page_table.shape[0]
    page = cache_kt.shape[-1]
    n_sb = qbd.shape[1]
    dec_seq = n_sb // SB_HEADS
    n_steps = n_tail // n_pg
    kd = SB_KV_HEADS * SB_HEAD_DIM

    def per_seq(a):
        return pl.BlockSpec((None,) + a.shape[1:], lambda b, j, pt, al: (b,) + (0,) * (a.ndim - 1))

    def paged(a, g):
        nd = a.ndim - 2
        return pl.BlockSpec(
            (None, None) + a.shape[2:],
            lambda b, j, pt, al: (layer, jnp.where(al[b] > 0, pt[b, (n_steps - 1 - j) * n_pg + g], 0)) + (0,) * nd)

    in_specs = [per_seq(qbd), pl.BlockSpec(tt.shape, lambda b, j, pt, al: (0, 0)), per_seq(acc_in), per_seq(carry_in)]
    operands = [qbd, tt, acc_in, carry_in]
    for cache in (cache_kt, cache_vt):
        for g in range(n_pg):
            in_specs.append(paged(cache, g))
            operands.append(cache)

    return pl.pallas_call(
        functools.partial(_sample_tail_kernel, n_pg=n_pg, dec_seq=dec_seq, page=page),
        grid_spec=pltpu.PrefetchScalarGridSpec(
            num_scalar_prefetch=2,
            grid=(dec_batch, n_steps),
            in_specs=in_specs,
            out_specs=pl.BlockSpec((None, kd, n_sb), lambda b, j, pt, al: (b, 0, 0)),
            scratch_shapes=[pltpu.VMEM((n_sb, LANES), F32), pltpu.VMEM((kd, n_sb), F32)],
        ),
        out_shape=jax.ShapeDtypeStruct((dec_batch, kd, n_sb), F32),
        compiler_params=_cparams(("parallel", "arbitrary")),
        name="sample_sb_tail",
    )(page_table, alive, *operands)


def _uv_kernel(x_ref, w_ref, o_ref):
    o_ref[...] = _mm(x_ref[...].astype(BF16), w_ref[...]).astype(BF16)


def _uv_call(o_lat, wuv_bd):
    n = o_lat.shape[0]
    return pl.pallas_call(
        _uv_kernel,
        grid=(1,),
        in_specs=[pl.BlockSpec(o_lat.shape, lambda i: (0, 0)), pl.BlockSpec(wuv_bd.shape, lambda i: (0, 0))],
        out_specs=pl.BlockSpec((n, MLA_HEADS * MLA_V), lambda i: (0, 0)),
        out_shape=jax.ShapeDtypeStruct((n, MLA_HEADS * MLA_V), BF16),
        compiler_params=_cparams(("arbitrary",)),
        name="sample_uv",
    )(o_lat, wuv_bd)


def _merge_kernel(*refs, n_prompt_tiles, with_router):
    if with_router:
        (oap_ref, obp_ref, oas_ref, obs_ref, g_ref, h_ref, wa_ref, wb_ref, wo_ref, gf_ref, wr_ref,
         hn_ref, xn_ref, ids_ref, wts_ref) = refs
    else:
        (oap_ref, obp_ref, oas_ref, obs_ref, g_ref, h_ref, wa_ref, wb_ref, wo_ref, gf_ref,
         hn_ref, xn_ref) = refs
    is_prompt = pl.program_id(0) < n_prompt_tiles
    oa = jnp.where(is_prompt, oap_ref[...], oas_ref[...])
    ob = jnp.where(is_prompt, obp_ref[...], obs_ref[...])
    br_a = _mm(oa, wa_ref[...])
    br_b = _mm(ob, wb_ref[...])
    mixed = g_ref[:, 0:D_MODEL].astype(F32) * br_a + g_ref[:, D_MODEL:2 * D_MODEL].astype(F32) * br_b
    hn = h_ref[...] + _mm(mixed.astype(BF16), wo_ref[...])
    hn_ref[...] = hn
    xn = hn * lax.rsqrt(jnp.mean(hn * hn, axis=-1, keepdims=True) + EPS) * gf_ref[...]
    xn_ref[...] = xn.astype(xn_ref.dtype)
    if with_router:
        logits = jnp.dot(xn, wr_ref[...], preferred_element_type=F32, precision=lax.Precision.HIGHEST)
        lane = lax.broadcasted_iota(jnp.int32, logits.shape, 1)
        valid = lane < N_EXPERTS
        lg = jnp.where(valid, logits, NEG_INF)
        m1 = jnp.max(lg, axis=1, keepdims=True)
        i1 = jnp.min(jnp.where(lg == m1, lane, LANES), axis=1, keepdims=True)
        lg2 = jnp.where(lane == i1, NEG_INF, lg)
        m2 = jnp.max(lg2, axis=1, keepdims=True)
        i2 = jnp.min(jnp.where(lg2 == m2, lane, LANES), axis=1, keepdims=True)
        e2 = jnp.exp(m2 - m1)
        w1 = 1.0 / (1.0 + e2)
        w2 = e2 / (1.0 + e2)
        ids_ref[...] = jnp.where(lane == 0, i1, jnp.where(lane == 1, i2, 0))
        wts_ref[...] = jnp.where(lane == 0, w1, jnp.where(lane == 1, w2, 0.0))


def _merge_call(oa_p, ob_p, oa_s, ob_s, gates, h, wa, wb, wo, gf, wr, tm):
    T = h.shape[0]
    n_p = oa_p.shape[0] // tm
    with_router = wr is not None
    row = lambda w: pl.BlockSpec((tm, w), lambda i: (i, 0))
    full = lambda a: pl.BlockSpec(a.shape, lambda i: (0, 0))
    p_spec = pl.BlockSpec((tm, oa_p.shape[1]), lambda i: (jnp.minimum(i, n_p - 1), 0))
    s_spec = pl.BlockSpec((tm, oa_s.shape[1]), lambda i: (jnp.maximum(i - n_p, 0), 0))
    in_specs = [p_spec, p_spec, s_spec, s_spec, row(2 * D_MODEL), row(D_MODEL), full(wa), full(wb), full(wo), full(gf)]
    operands = [oa_p, ob_p, oa_s, ob_s, gates, h, wa, wb, wo, gf]
    out_shape = [jax.ShapeDtypeStruct((T, D_MODEL), F32),
                 jax.ShapeDtypeStruct((T, D_MODEL), F32 if with_router else BF16)]
    out_specs = [row(D_MODEL), row(D_MODEL)]
    if with_router:
        in_specs.append(full(wr))
        operands.append(wr)
        out_shape += [jax.ShapeDtypeStruct((T, LANES), jnp.int32), jax.ShapeDtypeStruct((T, LANES), F32)]
        out_specs += [row(LANES), row(LANES)]
    return pl.pallas_call(
        functools.partial(_merge_kernel, n_prompt_tiles=n_p, with_router=with_router),
        grid=(T // tm,),
        in_specs=in_specs,
        out_specs=tuple(out_specs),
        out_shape=tuple(out_shape),
        compiler_params=_cparams(("parallel",)),
        name="merge",
    )(*operands)


def _swiglu_tile(x, wg, wu):
    a = _mm(x, wg)
    return (a * (1.0 / (1.0 + jnp.exp(-a))) * _mm(x, wu)).astype(BF16)


def _ffn_kernel(x_ref, h_ref, wg_ref, wu_ref, wd_ref, o_ref, acc_scr):
    f = pl.program_id(1)

    @pl.when(f == 0)
    def _():
        acc_scr[...] = h_ref[...]

    acc_scr[...] += _mm(_swiglu_tile(x_ref[...], wg_ref[...], wu_ref[...]), wd_ref[...])

    @pl.when(f == pl.num_programs(1) - 1)
    def _():
        o_ref[...] = acc_scr[...]


def _ffn_call(xn, h, wg, wu, wd, tm, tf):
    T = h.shape[0]
    d_ff = wg.shape[1]
    return pl.pallas_call(
        _ffn_kernel,
        grid=(T // tm, d_ff // tf),
        in_specs=[
            pl.BlockSpec((tm, D_MODEL), lambda i, f: (i, 0)),
            pl.BlockSpec((tm, D_MODEL), lambda i, f: (i, 0)),
            pl.BlockSpec((D_MODEL, tf), lambda i, f: (0, f)),
            pl.BlockSpec((D_MODEL, tf), lambda i, f: (0, f)),
            pl.BlockSpec((tf, D_MODEL), lambda i, f: (f, 0)),
        ],
        out_specs=pl.BlockSpec((tm, D_MODEL), lambda i, f: (i, 0)),
        out_shape=jax.ShapeDtypeStruct((T, D_MODEL), F32),
        scratch_shapes=[pltpu.VMEM((tm, D_MODEL), F32)],
        compiler_params=_cparams(("parallel", "arbitrary")),
        name="ffn_dense",
    )(xn, h, wg, wu, wd)


MOE_TILE = 512


def _row_copy(src_hbm, row, dst, r, sem):
    return pltpu.make_async_copy(src_hbm.at[pl.ds(row, 1)], dst.at[pl.ds(r, 1)], sem)


def _gather_kernel(idx_ref, src_hbm, o_ref, sem, *, rows):
    def start(r, c):
        _row_copy(src_hbm, idx_ref[0, r], o_ref, r, sem).start()
        return c

    lax.fori_loop(0, rows, start, 0, unroll=8)

    def wait(r, c):
        _row_copy(src_hbm, 0, o_ref, r, sem).wait()
        return c

    lax.fori_loop(0, rows, wait, 0, unroll=8)


def _gather_call(src, idx, rows):
    n_tiles = idx.shape[0]
    d = src.shape[1]
    return pl.pallas_call(
        functools.partial(_gather_kernel, rows=rows),
        grid=(n_tiles,),
        in_specs=[
            pl.BlockSpec((None, 1, rows), lambda i: (i, 0, 0), memory_space=pltpu.SMEM),
            pl.BlockSpec(memory_space=pl.ANY),
        ],
        out_specs=pl.BlockSpec((rows, d), lambda i: (i, 0)),
        out_shape=jax.ShapeDtypeStruct((n_tiles * rows, d), src.dtype),
        scratch_shapes=[pltpu.SemaphoreType.DMA(())],
        compiler_params=_cparams(("arbitrary",)),
        name="moe_gather",
    )(idx, src)


def _gmm_kernel(te_ref, tv_ref, x_ref, wg_ref, wu_ref, wd_ref, o_ref, xb_scr, acc_scr):
    del te_ref
    i = pl.program_id(0)
    f = pl.program_id(1)
    used = tv_ref[i] > 0

    @pl.when(used & (f == 0))
    def _():
        xb_scr[...] = x_ref[...].astype(BF16)

    @pl.when(used)
    def _():
        y = _mm(_swiglu_tile(xb_scr[...], wg_ref[...], wu_ref[...]), wd_ref[...])

        @pl.when(f == 0)
        def _():
            acc_scr[...] = y

        @pl.when(f > 0)
        def _():
            acc_scr[...] += y

    @pl.when(f == pl.num_programs(1) - 1)
    def _():
        @pl.when(used)
        def _():
            o_ref[...] = acc_scr[...]

        @pl.when(jnp.logical_not(used))
        def _():
            o_ref[...] = jnp.zeros(o_ref.shape, F32)


def _gmm_call(tile_expert, tile_used, x_sorted, wg, wu, wd, rows, tf):
    n_tiles = x_sorted.shape[0] // rows
    d_ff = wg.shape[2]
    n_f = d_ff // tf
    w_in = lambda i, f, te, tv: (te[i], 0, jnp.where(tv[i] > 0, f, n_f - 1))
    w_out = lambda i, f, te, tv: (te[i], jnp.where(tv[i] > 0, f, n_f - 1), 0)
    return pl.pallas_call(
        _gmm_kernel,
        grid_spec=pltpu.PrefetchScalarGridSpec(
            num_scalar_prefetch=2,
            grid=(n_tiles, n_f),
            in_specs=[
                pl.BlockSpec((rows, D_MODEL), lambda i, f, te, tv: (i, 0)),
                pl.BlockSpec((None, D_MODEL, tf), w_in),
                pl.BlockSpec((None, D_MODEL, tf), w_in),
                pl.BlockSpec((None, tf, D_MODEL), w_out),
            ],
            out_specs=pl.BlockSpec((rows, D_MODEL), lambda i, f, te, tv: (i, 0)),
            scratch_shapes=[pltpu.VMEM((rows, D_MODEL), BF16), pltpu.VMEM((rows, D_MODEL), F32)],
        ),
        out_shape=jax.ShapeDtypeStruct(x_sorted.shape, F32),
        compiler_params=_cparams(("parallel", "arbitrary")),
        name="ffn_experts",
    )(tile_expert, tile_used, x_sorted, wg, wu, wd)


def _combine_kernel(pos_ref, h_ref, w_ref, y_hbm, o_ref, buf, sem, *, rows):
    def start(r, c):
        _row_copy(y_hbm, pos_ref[0, r], buf, r, sem).start()
        return c

    lax.fori_loop(0, 2 * rows, start, 0, unroll=8)

    def wait(r, c):
        _row_copy(y_hbm, 0, buf, r, sem).wait()
        return c

    lax.fori_loop(0, 2 * rows, wait, 0, unroll=8)
    w = w_ref[...]
    o_ref[...] = h_ref[...] + (w[:, 0:1] * buf[0:rows, :] + w[:, 1:2] * buf[rows:2 * rows, :])


def _combine_call(pos, h, wts, y_sorted, rows):
    T = h.shape[0]
    return pl.pallas_call(
        functools.partial(_combine_kernel, rows=rows),
        grid=(T // rows,),
        in_specs=[
            pl.BlockSpec((None, 1, 2 * rows), lambda i: (i, 0, 0), memory_space=pltpu.SMEM),
            pl.BlockSpec((rows, D_MODEL), lambda i: (i, 0)),
            pl.BlockSpec((rows, LANES), lambda i: (i, 0)),
            pl.BlockSpec(memory_space=pl.ANY),
        ],
        out_specs=pl.BlockSpec((rows, D_MODEL), lambda i: (i, 0)),
        out_shape=jax.ShapeDtypeStruct((T, D_MODEL), F32),
        scratch_shapes=[pltpu.VMEM((2 * rows, D_MODEL), F32), pltpu.SemaphoreType.DMA(())],
        compiler_params=_cparams(("arbitrary",)),
        name="moe_combine",
    )(pos, h, wts, y_sorted)


def _route_plan(ids, rows):
    T = ids.shape[0]
    n_slots = 2 * T
    n_tiles = (n_slots + N_EXPERTS * (rows - 1) + rows - 1) // rows
    e_slot = ids.reshape(n_slots)
    onehot = (e_slot[:, None] == jnp.arange(N_EXPERTS, dtype=jnp.int32)[None, :]).astype(jnp.int32)
    rank = jnp.sum((jnp.cumsum(onehot, axis=0) - onehot) * onehot, axis=1)
    counts = jnp.sum(onehot, axis=0)
    padded = ((counts + rows - 1) // rows) * rows
    ends = jnp.cumsum(padded)
    starts = ends - padded
    dest = starts[e_slot] + rank
    src = jnp.zeros((n_tiles * rows,), jnp.int32).at[dest].set(jnp.arange(n_slots, dtype=jnp.int32) // 2)
    tile_start = jnp.arange(n_tiles, dtype=jnp.int32) * rows
    tile_expert = jnp.minimum(jnp.sum((tile_start[:, None] >= ends[None, :]).astype(jnp.int32), axis=1),
                              N_EXPERTS - 1)
    tile_used = (tile_start < ends[-1]).astype(jnp.int32)
    return src.reshape(n_tiles, 1, rows), tile_expert, tile_used, dest.reshape(T, 2)


def _moe_call(xn32, h, ids, wts, wg, wu, wd, tm, tf):
    T = h.shape[0]
    src, tile_expert, tile_used, dest = _route_plan(ids[:, 0:2], MOE_TILE)
    x_sorted = _gather_call(xn32, src, MOE_TILE)
    y_sorted = _gmm_call(tile_expert, tile_used, x_sorted, wg, wu, wd, MOE_TILE, tf)
    pos = dest.reshape(T // tm, tm, 2).transpose(0, 2, 1).reshape(T // tm, 1, 2 * tm)
    return _combine_call(pos, h, wts, y_sorted, tm)


def _head_block_cols(w_rope, w_nope):
    pad = jnp.zeros(w_rope.shape[:-1] + (LANES - MLA_ROPE - MLA_NOPE,), w_rope.dtype)
    blk = jnp.concatenate([w_rope, w_nope, pad], axis=-1)
    return blk.reshape(blk.shape[:-2] + (blk.shape[-2] * LANES,))


def _frame(rope=None, nope=None):
    v = jnp.zeros((LANES,), F32)
    if rope is not None:
        v = v.at[0:MLA_ROPE].set(rope)
    if nope is not None:
        v = v.at[NOPE_LO:NOPE_HI].set(nope)
    return v


def _row(v):
    return jnp.zeros((D_MODEL,), F32).at[0:v.shape[0]].set(v)


def _rope_table(pos):
    inv_freq = ROPE_THETA ** (-jnp.arange(HALF_ROPE, dtype=F32) / HALF_ROPE)
    ang = pos.astype(F32)[:, None] * inv_freq[None, :]
    cos, sin = jnp.cos(ang), jnp.sin(ang)
    n = pos.shape[0]
    z16 = jnp.zeros((n, HALF_ROPE), F32)
    cos_t = jnp.concatenate([cos, cos, jnp.ones((n, LANES - MLA_ROPE), F32)], axis=1)
    sina = jnp.concatenate([-sin, jnp.zeros((n, LANES - HALF_ROPE), F32)], axis=1)
    sinb = jnp.concatenate([z16, sin, jnp.zeros((n, LANES - MLA_ROPE), F32)], axis=1)
    return jnp.concatenate([cos_t, sina, sinb], axis=1)


def kernel(x_prompt, x_sample, cache_mla_latent, cache_mla_krope, cache_sb_k, cache_sb_v, page_table, g_attn_norm, w_in, g_q_lat, w_q_up, g_q_nope, g_q_rope, g_kv_lat, g_k_rope, w_uk, g_k_nope, w_uv, w_mla_out, w_sb_out, w_o, g_ffn_norm, w_ffn_gate, w_ffn_up, w_ffn_down, w_router, w_exp_gate, w_exp_up, w_exp_down):
    batch, seq, _ = x_prompt.shape
    dec_batch, dec_seq, _ = x_sample.shape
    depth = w_in.shape[0]
    n_pages = page_table.shape[1]
    page = cache_mla_latent.shape[2]
    past_len = n_pages * page
    t_p = batch * seq
    t_s = dec_batch * dec_seq
    T = t_p + t_s
    assert page == LANES and dec_seq <= page

    tm = _pick(math.gcd(t_p, t_s), (512, 256, 128, 64, 32, 16))
    tq = _pick(seq, (512, 256, 128))
    ck = min(SB_CHUNK, tq)
    n_pg = _pick(n_pages, (16, 8, 4, 2, 1))
    n_head = min(4, n_pages)
    n_tail = n_pages - n_head
    n_pg_tail = _pick(n_tail, (15, 12, 10, 8, 6, 5, 4, 3, 2, 1)) if n_tail else 0
    tm_ffn = _pick(T, (768, 512, 384, 256, 192, 128, 64, 32, 16))

    pos = jnp.concatenate([
        jnp.tile(jnp.arange(seq, dtype=jnp.int32), batch),
        jnp.tile(past_len + jnp.arange(dec_seq, dtype=jnp.int32), dec_batch)])
    tab = _rope_table(pos)
    tt_prompt = _suffix_matrix(ck)
    tt_page = _suffix_matrix(page)

    cache_krt = jnp.transpose(cache_mla_krope, (0, 1, 3, 2))
    cache_kt = jnp.transpose(cache_sb_k, (0, 1, 3, 4, 2))
    cache_vt = jnp.transpose(cache_sb_v, (0, 1, 3, 4, 2))

    h = jnp.concatenate([x_prompt.reshape(t_p, D_MODEL), x_sample.reshape(t_s, D_MODEL)], axis=0)
    outs = [[] for _ in range(8)]
    sizes = np.cumsum([Q_LORA, KV_LORA, MLA_ROPE, SB_HEADS * SB_HEAD_DIM, SB_KV_HEADS * SB_HEAD_DIM,
                       SB_KV_HEADS * SB_HEAD_DIM, D_MODEL])

    for l in range(depth):
        w_ql, w_c, w_kr, w_sq, w_sk, w_sv, w_ga, w_gb = jnp.split(w_in[l], sizes, axis=1)
        w_kr_pad = jnp.concatenate([w_kr, jnp.zeros((D_MODEL, LANES - MLA_ROPE), F32)], axis=1)
        w_all = jnp.concatenate([w_ql, w_c, w_kr_pad, w_sq * SB_SCALE, w_sk, w_sv, w_ga, w_gb], axis=1).astype(BF16)
        wq3 = w_q_up[l].reshape(Q_LORA, MLA_HEADS, MLA_NOPE + MLA_ROPE)
        wqu = _head_block_cols(wq3[..., MLA_NOPE:], wq3[..., :MLA_NOPE]).astype(BF16)
        wuk = _head_block_cols(jnp.zeros((KV_LORA, MLA_HEADS, MLA_ROPE), F32), w_uk[l]).astype(BF16)
        gv = jnp.stack([
            g_attn_norm[l], _row(g_q_lat[l]),
            _row(_frame(rope=g_q_rope[l] * MLA_SCALE)), _row(_frame(nope=g_q_nope[l] * MLA_SCALE)),
            _row(g_kv_lat[l]), _row(_frame(rope=g_k_rope[l])), _row(_frame(nope=g_k_nope[l])),
            jnp.zeros((D_MODEL,), F32)])
        wuv_bd = jnp.zeros((MLA_HEADS * KV_LORA, MLA_HEADS * MLA_V), F32)
        for hh in range(MLA_HEADS):
            wuv_bd = wuv_bd.at[hh * KV_LORA:(hh + 1) * KV_LORA, hh * MLA_V:(hh + 1) * MLA_V].set(w_uv[l][:, hh, :])
        wuv_bd = wuv_bd.astype(BF16)

        (qcat, kcat, c, cbf, kr, sq, sk, sv, skbf, svbf, gates) = _proj_call(h, tab, gv, w_all, wqu, wuk, tm)

        oa_p = _mla_call(qcat, kcat, cbf, wuv_bd, batch, seq, tq)
        k_heads = skbf[:t_p].reshape(batch, seq, SB_KV_HEADS, SB_HEAD_DIM).transpose(0, 2, 1, 3)
        v_heads = svbf[:t_p].reshape(batch, seq, SB_KV_HEADS, SB_HEAD_DIM).transpose(0, 2, 1, 3)
        v_heads = jnp.concatenate([v_heads, v_heads], axis=-1)
        ob_p = _sb_call(sq, k_heads, v_heads, tt_prompt, batch, seq, tq, ck)

        q_s = qcat[t_p:]
        wabs = jnp.zeros((MLA_HEADS, LANES, KV_LORA), F32)
        wabs = wabs.at[:, NOPE_LO:NOPE_HI, :].set(
            jnp.transpose(w_uk[l] * g_k_nope[l][None, None, :], (1, 2, 0)))
        qabs = _absorb_call(q_s, wabs.astype(BF16))
        qabs = qabs.reshape(dec_batch, dec_seq * MLA_HEADS, KV_LORA)
        qr = q_s.reshape(dec_batch, dec_seq * MLA_HEADS, LANES)[:, :, 0:MLA_ROPE]
        sq_s = sq[t_p:].reshape(dec_batch, dec_seq, SB_HEADS, SB_HEAD_DIM).transpose(0, 2, 1, 3)
        sq_s = sq_s.reshape(dec_batch, SB_KV_HEADS, SB_GROUP * dec_seq, SB_HEAD_DIM)
        eye = jnp.eye(SB_KV_HEADS, dtype=BF16)
        qbd = (sq_s[:, :, :, None, :] * eye[None, :, None, :, None]).reshape(
            dec_batch, SB_HEADS * dec_seq, SB_KV_HEADS * SB_HEAD_DIM)
        wukt = jnp.transpose(w_uk[l], (1, 2, 0)).reshape(MLA_HEADS * MLA_NOPE, KV_LORA).astype(BF16)
        pad_k = page - dec_seq
        nlat = jnp.pad(c[t_p:].reshape(dec_batch, dec_seq, KV_LORA), ((0, 0), (0, pad_k), (0, 0)))
        nkr = jnp.pad(kr[t_p:].reshape(dec_batch, dec_seq, MLA_ROPE).transpose(0, 2, 1), ((0, 0), (0, 0), (0, pad_k)))
        nk = jnp.pad(sk[t_p:].reshape(dec_batch, dec_seq, -1).transpose(0, 2, 1), ((0, 0), (0, 0), (0, pad_k)))
        nv = jnp.pad(sv[t_p:].reshape(dec_batch, dec_seq, -1).transpose(0, 2, 1), ((0, 0), (0, 0), (0, pad_k)))
        o_lat_s, o_sbt, sb_carry = _sample_call(l, page_table, qabs, qr, qbd, wukt, tt_page, nlat, nkr, nk, nv,
                                                cache_mla_latent, cache_krt, cache_kt, cache_vt, n_pg, n_head)
        if n_tail > 0:
            alive = (jnp.max(sb_carry, axis=(1, 2)) >= SB_DEAD_LOG).astype(jnp.int32)
            o_sbt = lax.cond(
                jnp.max(alive) > 0,
                functools.partial(_sample_tail_call, l, page_table, alive, qbd, tt_page, o_sbt, sb_carry,
                                  cache_kt, cache_vt, n_tail, n_pg_tail),
                lambda: o_sbt)
        o_lat_s = o_lat_s.reshape(dec_batch, dec_seq, MLA_HEADS, KV_LORA).reshape(t_s, MLA_HEADS * KV_LORA)
        oa_s = _uv_call(o_lat_s, wuv_bd)
        o5 = o_sbt.reshape(dec_batch, SB_KV_HEADS, SB_HEAD_DIM, SB_KV_HEADS, SB_GROUP, dec_seq)
        o5 = jnp.stack([o5[:, kv, :, kv] for kv in range(SB_KV_HEADS)], axis=1)
        ob_s = o5.transpose(0, 4, 1, 3, 2).reshape(t_s, SB_HEADS * SB_HEAD_DIM).astype(BF16)

        j = l // 2
        moe = l % 2 == 1
        wr = None
        if moe:
            wr = jnp.concatenate([w_router[j], jnp.zeros((D_MODEL, LANES - N_EXPERTS), F32)], axis=1)
        res = _merge_call(oa_p, ob_p, oa_s, ob_s, gates, h,
                          w_mla_out[l].astype(BF16), w_sb_out[l].astype(BF16), w_o[l].astype(BF16),
                          g_ffn_norm[l][None, :], wr, tm)
        if moe:
            hn, xn32, route_ids, route_wts = res
            h = _moe_call(xn32, hn, route_ids, route_wts, w_exp_gate[j].astype(BF16), w_exp_up[j].astype(BF16),
                          w_exp_down[j].astype(BF16), _pick(T, (256, 128, 64, 32, 16, 8)), 512)
        else:
            hn, xn = res
            d_ff = w_ffn_gate.shape[2]
            h = _ffn_call(xn, hn, w_ffn_gate[j].astype(BF16), w_ffn_up[j].astype(BF16),
                          w_ffn_down[j].astype(BF16), tm_ffn, _pick(d_ff, (1408, 512, 256, 128)))

        for idx, a in enumerate((c, kr, sk, sv)):
            outs[idx].append(a[:t_p])
            outs[4 + idx].append(a[t_p:])

    def stack_p(xs, tail):
        return jnp.stack(xs).reshape((depth, batch, seq) + tail)

    def stack_s(xs, tail):
        return jnp.stack(xs).reshape((depth, dec_batch, dec_seq) + tail)

    kv_tail = (SB_KV_HEADS, SB_HEAD_DIM)
    return (h[:t_p].reshape(batch, seq, D_MODEL), h[t_p:].reshape(dec_batch, dec_seq, D_MODEL),
            stack_p(outs[0], (KV_LORA,)), stack_p(outs[1], (MLA_ROPE,)),
            stack_p(outs[2], kv_tail), stack_p(outs[3], kv_tail),
            stack_s(outs[4], (KV_LORA,)), stack_s(outs[5], (MLA_ROPE,)),
            stack_s(outs[6], kv_tail), stack_s(outs[7], kv_tail))
```

```python
import functools
import math

import jax
import jax.numpy as jnp
import numpy as np
from jax import lax
from jax.experimental import pallas as pl
from jax.experimental.pallas import tpu as pltpu

D_MODEL = 1024
MLA_HEADS = 8
MLA_NOPE = 64
MLA_ROPE = 32
MLA_V = 64
Q_LORA = 256
KV_LORA = 128
ROPE_THETA = 10000.0
MLA_SCALE = (MLA_NOPE + MLA_ROPE) ** -0.5
SB_HEADS = 8
SB_KV_HEADS = 4
SB_GROUP = SB_HEADS // SB_KV_HEADS
SB_HEAD_DIM = 64
SB_SCALE = SB_HEAD_DIM ** -0.5
N_EXPERTS = 8
EPS = 1e-6
NEG_INF = -1e30

LANES = 128
HALF_ROPE = MLA_ROPE // 2
NOPE_LO, NOPE_HI = MLA_ROPE, MLA_ROPE + MLA_NOPE
VMEM_LIMIT = 56 * 1024 * 1024

F32 = jnp.float32
BF16 = jnp.bfloat16


def _pick(n, candidates):
    for c in candidates:
        if n % c == 0:
            return c
    return n


def _mm(a, b):
    return jnp.dot(a, b, preferred_element_type=F32)


def _mm_nt(a, b):
    return lax.dot_general(a, b, (((1,), (1,)), ((), ())), preferred_element_type=F32)


def _cparams(sem):
    return pltpu.CompilerParams(dimension_semantics=sem, vmem_limit_bytes=VMEM_LIMIT)


def _staggered(gens, offset=1):
    pending = list(gens)
    active = []
    rnd = 0
    while pending or active:
        if pending and (offset == 0 or rnd % offset == 0):
            if offset == 0:
                active.extend(pending)
                pending = []
            else:
                active.append(pending.pop(0))
        for g in list(active):
            try:
                next(g)
            except StopIteration:
                active.remove(g)
        rnd += 1


_C_QLAT, _C_CKV, _C_KR, _C_SQ, _C_SK, _C_SV, _C_G, _C_END = 0, 256, 384, 512, 1024, 1280, 1536, 3584


def _rope_rotate(y, cos, sina, sinb):
    return y * cos + pltpu.roll(y, LANES - HALF_ROPE, 1) * sina + pltpu.roll(y, HALF_ROPE, 1) * sinb


def _proj_kernel(h_ref, tab_ref, gv_ref, w_ref, wqu_ref, wuk_ref,
                 qcat_ref, kcat_ref, c_ref, cbf_ref, kr_ref, sq_ref, sk_ref, sv_ref,
                 skbf_ref, svbf_ref, gates_ref):
    x = h_ref[...]
    g_attn = gv_ref[0:1, :]
    xn = (x * lax.rsqrt(jnp.mean(x * x, axis=-1, keepdims=True) + EPS) * g_attn).astype(BF16)

    cos = tab_ref[:, 0:LANES]
    sina = tab_ref[:, LANES:2 * LANES]
    sinb = tab_ref[:, 2 * LANES:3 * LANES]
    lane = lax.broadcasted_iota(jnp.int32, (1, LANES), 1)
    m_rope = (lane < MLA_ROPE).astype(F32)
    m_nope = ((lane >= NOPE_LO) & (lane < NOPE_HI)).astype(F32)

    q_lat = _mm(xn, w_ref[:, _C_QLAT:_C_CKV])
    g_ql = gv_ref[1:2, 0:Q_LORA]
    ql = (q_lat * lax.rsqrt(jnp.mean(q_lat * q_lat, axis=-1, keepdims=True) + EPS) * g_ql).astype(BF16)
    q = _mm(ql, wqu_ref[...])
    g_qr = gv_ref[2:3, 0:LANES]
    g_qn = gv_ref[3:4, 0:LANES]
    for h in range(MLA_HEADS):
        xh = q[:, h * LANES:(h + 1) * LANES]
        x2 = xh * xh
        ss_r = jnp.sum(x2 * m_rope, axis=-1, keepdims=True)
        ss_n = jnp.sum(x2 * m_nope, axis=-1, keepdims=True)
        sc = lax.rsqrt(ss_r * (1.0 / MLA_ROPE) + EPS) * g_qr + lax.rsqrt(ss_n * (1.0 / MLA_NOPE) + EPS) * g_qn
        y = _rope_rotate(xh * sc, cos, sina, sinb)
        qcat_ref[:, h * LANES:(h + 1) * LANES] = y.astype(BF16)

    c_raw = _mm(xn, w_ref[:, _C_CKV:_C_KR])
    g_kvl = gv_ref[4:5, 0:KV_LORA]
    c = c_raw * lax.rsqrt(jnp.mean(c_raw * c_raw, axis=-1, keepdims=True) + EPS) * g_kvl
    c_ref[...] = c
    cbf = c.astype(BF16)
    cbf_ref[...] = cbf

    kr_raw = _mm(xn, w_ref[:, _C_KR:_C_SQ])
    g_kr = gv_ref[5:6, 0:LANES]
    ss = jnp.sum(kr_raw * kr_raw, axis=-1, keepdims=True)
    krn = _rope_rotate(kr_raw * (lax.rsqrt(ss * (1.0 / MLA_ROPE) + EPS) * g_kr), cos, sina, sinb)
    kr_ref[...] = krn[:, 0:MLA_ROPE]

    kp = _mm(cbf, wuk_ref[...])
    g_kn = gv_ref[6:7, 0:LANES]
    for h in range(MLA_HEADS):
        kh = kp[:, h * LANES:(h + 1) * LANES]
        ss_n = jnp.sum(kh * kh, axis=-1, keepdims=True)
        kn = kh * (lax.rsqrt(ss_n * (1.0 / MLA_NOPE) + EPS) * g_kn)
        kcat_ref[:, h * LANES:(h + 1) * LANES] = (kn + krn).astype(BF16)

    sq_ref[...] = _mm(xn, w_ref[:, _C_SQ:_C_SK]).astype(BF16)
    sk = _mm(xn, w_ref[:, _C_SK:_C_SV])
    sk_ref[...] = sk
    skbf_ref[...] = sk.astype(BF16)
    sv = _mm(xn, w_ref[:, _C_SV:_C_G])
    sv_ref[...] = sv
    svbf_ref[...] = sv.astype(BF16)

    g = _mm(xn, w_ref[:, _C_G:_C_END])
    gates_ref[...] = (1.0 / (1.0 + jnp.exp(-g))).astype(BF16)


def _proj_call(h, tab, gv, w_all, wqu, wuk, tm):
    T = h.shape[0]
    row = lambda w: pl.BlockSpec((tm, w), lambda i: (i, 0))
    full = lambda a: pl.BlockSpec(a.shape, lambda i: (0, 0))
    out_shapes = (
        jax.ShapeDtypeStruct((T, MLA_HEADS * LANES), BF16),
        jax.ShapeDtypeStruct((T, MLA_HEADS * LANES), BF16),
        jax.ShapeDtypeStruct((T, KV_LORA), F32),
        jax.ShapeDtypeStruct((T, KV_LORA), BF16),
        jax.ShapeDtypeStruct((T, MLA_ROPE), F32),
        jax.ShapeDtypeStruct((T, SB_HEADS * SB_HEAD_DIM), BF16),
        jax.ShapeDtypeStruct((T, SB_KV_HEADS * SB_HEAD_DIM), F32),
        jax.ShapeDtypeStruct((T, SB_KV_HEADS * SB_HEAD_DIM), F32),
        jax.ShapeDtypeStruct((T, SB_KV_HEADS * SB_HEAD_DIM), BF16),
        jax.ShapeDtypeStruct((T, SB_KV_HEADS * SB_HEAD_DIM), BF16),
        jax.ShapeDtypeStruct((T, 2 * D_MODEL), BF16),
    )
    return pl.pallas_call(
        _proj_kernel,
        grid=(T // tm,),
        in_specs=[row(D_MODEL), row(3 * LANES), full(gv), full(w_all), full(wqu), full(wuk)],
        out_specs=tuple(row(s.shape[1]) for s in out_shapes),
        out_shape=out_shapes,
        compiler_params=_cparams(("parallel",)),
        name="proj",
    )(h, tab, gv, w_all, wqu, wuk)


def _mla_kernel(qi_tab, ki_tab, q_ref, k_ref, v_ref, wuv_ref, o_ref, m_scr, acc_scr, *, tq):
    t = pl.program_id(1)
    qi = qi_tab[t]
    ki = ki_tab[t]

    @pl.when(ki == 0)
    def _():
        m_scr[...] = jnp.full(m_scr.shape, NEG_INF, F32)
        acc_scr[...] = jnp.zeros(acc_scr.shape, F32)

    def step(masked):
        v = v_ref[...]
        if masked:
            row = lax.broadcasted_iota(jnp.int32, (tq, tq), 0)
            col = lax.broadcasted_iota(jnp.int32, (tq, tq), 1)
            keep = col <= row

        for h in range(MLA_HEADS):
            s = _mm_nt(q_ref[:, h * LANES:(h + 1) * LANES], k_ref[:, h * LANES:(h + 1) * LANES])
            if masked:
                s = jnp.where(keep, s, NEG_INF)
            m_prev = m_scr[h]
            m_new = jnp.maximum(m_prev, jnp.max(s, axis=1, keepdims=True))
            alpha = jnp.exp(m_prev - m_new)
            p = jnp.exp(s - m_new[:, 0:1])
            acc_scr[h] = jnp.concatenate([alpha, alpha], axis=1) * acc_scr[h] + _mm(p.astype(BF16), v)
            m_scr[h] = m_new

    @pl.when(ki < qi)
    def _():
        step(False)

    @pl.when(ki == qi)
    def _():
        step(True)
        o_lat = jnp.concatenate(
            [(acc_scr[h, :, 0:KV_LORA] / acc_scr[h, :, KV_LORA:2 * KV_LORA]).astype(BF16)
             for h in range(MLA_HEADS)], axis=1)
        o_ref[...] = _mm(o_lat, wuv_ref[...]).astype(BF16)


def _tri_tables(nq, descending):
    qi, ki = [], []
    for i in range(nq):
        ks = range(i, -1, -1) if descending else range(i + 1)
        for j in ks:
            qi.append(i)
            ki.append(j)
    return jnp.asarray(np.array(qi, np.int32)), jnp.asarray(np.array(ki, np.int32))


def _mla_call(qcat, kcat, cbf, wuv_bd, batch, seq, tq):
    nq = seq // tq
    qi_tab, ki_tab = _tri_tables(nq, descending=False)
    n_tri = int(qi_tab.shape[0])
    return pl.pallas_call(
        functools.partial(_mla_kernel, tq=tq),
        grid_spec=pltpu.PrefetchScalarGridSpec(
            num_scalar_prefetch=2,
            grid=(batch, n_tri),
            in_specs=[
                pl.BlockSpec((tq, MLA_HEADS * LANES), lambda b, t, qt, kt: (b * nq + qt[t], 0)),
                pl.BlockSpec((tq, MLA_HEADS * LANES), lambda b, t, qt, kt: (b * nq + kt[t], 0)),
                pl.BlockSpec((tq, 2 * KV_LORA), lambda b, t, qt, kt: (b * nq + kt[t], 0)),
                pl.BlockSpec(wuv_bd.shape, lambda b, t, qt, kt: (0, 0)),
            ],
            out_specs=pl.BlockSpec((tq, MLA_HEADS * MLA_V), lambda b, t, qt, kt: (b * nq + qt[t], 0)),
            scratch_shapes=[
                pltpu.VMEM((MLA_HEADS, tq, LANES), F32),
                pltpu.VMEM((MLA_HEADS, tq, 2 * KV_LORA), F32),
            ],
        ),
        out_shape=jax.ShapeDtypeStruct((batch * seq, MLA_HEADS * MLA_V), BF16),
        compiler_params=_cparams(("parallel", "arbitrary")),
        name="mla_prompt",
    )(qi_tab, ki_tab, qcat, kcat, jnp.concatenate([cbf, jnp.ones_like(cbf)], axis=1), wuv_bd)


SB_CHUNK = 256


def _suffix_matrix(n):
    j = np.arange(n)[:, None]
    s = np.arange(n)[None, :]
    t = (j > s).astype(np.float32)
    return jnp.asarray(np.concatenate([t, t], axis=0), dtype=BF16)


def _sb_terms(z):
    lp = jnp.log(1.0 + jnp.exp(-jnp.abs(z)))
    log_beta = jnp.minimum(z, 0.0) - lp
    return log_beta, log_beta - z


def _split_bf16(x):
    hi = x.astype(BF16)
    return hi, (x - hi.astype(F32)).astype(BF16)


SB_DEAD_LOG = -104.0


def _sb_chunk(q2, k, v, tt, carry, keep, tq):
    res = []

    def gen():
        z = _mm_nt(q2, k)
        yield
        log_beta, l1 = _sb_terms(z)
        if keep is not None:
            l1 = jnp.where(keep, l1, 0.0)
        hi, lo = _split_bf16(l1)
        res.append(carry + jnp.sum(l1, axis=1, keepdims=True))
        yield
        after = _mm(jnp.concatenate([hi, lo], axis=1), tt)
        yield
        w = jnp.exp(log_beta + after + carry[:, 0:1])
        if keep is not None:
            w = jnp.where(keep, w, 0.0)
        w = w.astype(BF16)
        yield
        o2 = _mm(w, v)
        first_head = lax.broadcasted_iota(jnp.int32, (1, LANES), 1) < SB_HEAD_DIM
        res.append(jnp.where(first_head, o2[0:tq], o2[tq:2 * tq]))

    return gen(), res


def _sb_kernel(q_ref, k_ref, v_ref, tt_ref, o_ref, carry_scr, acc_scr, *, tq, ck):
    qi = pl.program_id(2)
    n_chunks = tq // ck
    tt = tt_ref[...]
    q2 = jnp.concatenate([q_ref[:, g * SB_HEAD_DIM:(g + 1) * SB_HEAD_DIM] for g in range(SB_GROUP)], axis=0)

    base = pl.multiple_of(qi * tq, tq)
    row = lax.broadcasted_iota(jnp.int32, (SB_GROUP * tq, ck), 0) & (tq - 1)
    col0 = lax.broadcasted_iota(jnp.int32, (SB_GROUP * tq, ck), 1)
    carry = jnp.zeros((SB_GROUP * tq, LANES), F32)
    gens, results = [], []
    carry_box = [carry]

    def diag_chunk(c):
        k = k_ref[pl.ds(base + c * ck, ck), :]
        v = v_ref[pl.ds(base + c * ck, ck), :]
        z = _mm_nt(q2, k)
        yield
        keep = (col0 + c * ck) < row
        log_beta, l1 = _sb_terms(z)
        l1 = jnp.where(keep, l1, 0.0)
        hi, lo = _split_bf16(l1)
        carry_in = carry_box[-1]
        carry_box.append(carry_in + jnp.sum(l1, axis=1, keepdims=True))
        yield
        after = _mm(jnp.concatenate([hi, lo], axis=1), tt)
        yield
        w = jnp.where(keep, jnp.exp(log_beta + after + carry_in[:, 0:1]), 0.0).astype(BF16)
        yield
        o2 = _mm(w, v)
        first_head = lax.broadcasted_iota(jnp.int32, (1, LANES), 1) < SB_HEAD_DIM
        results.append(jnp.where(first_head, o2[0:tq], o2[tq:2 * tq]))

    _staggered([diag_chunk(c) for c in range(n_chunks - 1, -1, -1)])
    carry_scr[...] = carry_box[-1]
    acc_scr[...] = functools.reduce(lambda a, b: a + b, results)

    def alive_of(c):
        return (jnp.max(c) >= SB_DEAD_LOG).astype(jnp.int32)

    def cond(st):
        j, alive = st
        return (j >= 0) & (alive > 0)

    def body(st):
        j, _ = st
        start = pl.multiple_of(j * ck, ck)
        gen, res = _sb_chunk(q2, k_ref[pl.ds(start, ck), :], v_ref[pl.ds(start, ck), :], tt,
                             carry_scr[...], None, tq)
        for _ in gen:
            pass
        carry_new, out = res
        carry_scr[...] = carry_new
        acc_scr[...] += out
        return j - 1, alive_of(carry_new)

    lax.while_loop(cond, body, (qi * n_chunks - 1, alive_of(carry_box[-1])))
    o_ref[...] = acc_scr[...].astype(BF16)


def _sb_call(sq, k_heads, v_heads, tt, batch, seq, tq, ck):
    nq = seq // tq
    gw = SB_GROUP * SB_HEAD_DIM
    return pl.pallas_call(
        functools.partial(_sb_kernel, tq=tq, ck=ck),
        grid=(batch, SB_KV_HEADS, nq),
        in_specs=[
            pl.BlockSpec((tq, gw), lambda b, kv, i: (b * nq + i, kv)),
            pl.BlockSpec((None, None, seq, SB_HEAD_DIM), lambda b, kv, i: (b, kv, 0, 0)),
            pl.BlockSpec((None, None, seq, gw), lambda b, kv, i: (b, kv, 0, 0)),
            pl.BlockSpec(tt.shape, lambda b, kv, i: (0, 0)),
        ],
        out_specs=pl.BlockSpec((tq, gw), lambda b, kv, i: (b * nq + i, kv)),
        scratch_shapes=[
            pltpu.VMEM((SB_GROUP * tq, LANES), F32),
            pltpu.VMEM((tq, gw), F32),
        ],
        out_shape=jax.ShapeDtypeStruct((batch * seq, SB_HEADS * SB_HEAD_DIM), BF16),
        compiler_params=_cparams(("parallel", "parallel", "arbitrary")),
        name="sb_prompt",
    )(sq, k_heads, v_heads, tt)


def _absorb_kernel(q_ref, w_ref, o_ref):
    for h in range(MLA_HEADS):
        o_ref[:, h * LANES:(h + 1) * LANES] = _mm(q_ref[:, h * LANES:(h + 1) * LANES], w_ref[h]).astype(BF16)


def _absorb_call(q_s, wabs):
    n = q_s.shape[0]
    return pl.pallas_call(
        _absorb_kernel,
        grid=(1,),
        in_specs=[pl.BlockSpec(q_s.shape, lambda i: (0, 0)), pl.BlockSpec(wabs.shape, lambda i: (0, 0, 0))],
        out_specs=pl.BlockSpec((n, MLA_HEADS * LANES), lambda i: (0, 0)),
        out_shape=jax.ShapeDtypeStruct((n, MLA_HEADS * LANES), BF16),
        compiler_params=_cparams(("arbitrary",)),
        name="absorb_q",
    )(q_s, wabs)


def _sample_sb_tile(qbd, tt, kt_pages, vt_pages, carry_in, keep, n_sb):
    res = []

    def gen():
        n = len(kt_pages)
        z = jnp.concatenate([_mm(qbd, p.astype(BF16)) for p in kt_pages], axis=0)
        yield
        log_beta, l1 = _sb_terms(z)
        if keep is not None:
            l1 = jnp.where(keep, l1, 0.0)
        hi, lo = _split_bf16(l1)
        tot = jnp.sum(l1, axis=1, keepdims=True)
        run = carry_in
        carries = [None] * n
        for i in range(n - 1, -1, -1):
            carries[i] = run
            run = run + tot[i * n_sb:(i + 1) * n_sb]
        res.append(run)
        carry = jnp.concatenate([c[:, 0:1] for c in carries], axis=0)
        yield
        after = _mm(jnp.concatenate([hi, lo], axis=1), tt)
        yield
        w = jnp.exp(log_beta + after + carry)
        if keep is not None:
            w = jnp.where(keep, w, 0.0)
        wl = jnp.concatenate([w[i * n_sb:(i + 1) * n_sb].astype(BF16) for i in range(n)], axis=1)
        vt = jnp.concatenate([p.astype(BF16) for p in vt_pages], axis=1)
        res.append(_mm_nt(vt, wl))

    return gen(), res


def _sample_kernel(*refs, n_pg, n_head, dec_seq, page):
    pt_ref = refs[0]
    del pt_ref
    qabs_ref, qr_ref, qbd_ref, wukt_ref, tt_ref, nlat_ref, nkr_ref, nk_ref, nv_ref = refs[1:10]
    lat_refs = refs[10:10 + n_pg]
    kr_refs = refs[10 + n_pg:10 + 2 * n_pg]
    o = 10 + 2 * n_pg
    k_refs = refs[o:o + n_head]
    v_refs = refs[o + n_head:o + 2 * n_head]
    o += 2 * n_head
    olat_ref, osb_ref, ocarry_ref = refs[o:o + 3]
    wst_scr, m_scr, l_scr, acc_scr = refs[o + 3:]

    j = pl.program_id(1)
    n_rows = MLA_HEADS * dec_seq
    n_sb = SB_HEADS * dec_seq
    kd = SB_KV_HEADS * SB_HEAD_DIM

    def mla_tile(c_pages, krt_pages, mask):
        cb = jnp.concatenate([p.astype(BF16) for p in c_pages], axis=0)
        a = _mm_nt(wst_scr[...], cb)
        n_keys = a.shape[1]
        yield
        krt = jnp.concatenate([p.astype(BF16) for p in krt_pages], axis=1)
        ssq = jnp.concatenate(
            [jnp.sum(jnp.square(a[h * MLA_NOPE:(h + 1) * MLA_NOPE]), axis=0, keepdims=True)
             for h in range(MLA_HEADS)], axis=0)
        r = lax.rsqrt(ssq * (1.0 / MLA_NOPE) + EPS)
        sn = a[MLA_HEADS * MLA_NOPE:]
        s = sn * jnp.concatenate([r] * dec_seq, axis=0) + _mm(qr_ref[...], krt)
        if mask is not None:
            s = jnp.where(mask(n_rows, n_keys, MLA_HEADS, True), s, NEG_INF)
        yield
        m_prev = m_scr[...]
        m_new = jnp.maximum(m_prev, jnp.max(s, axis=1, keepdims=True))
        alpha = jnp.exp(m_prev - m_new)
        p = jnp.exp(s - m_new[:, 0:1])
        l_scr[...] = alpha * l_scr[...] + jnp.sum(p, axis=1, keepdims=True)
        m_scr[...] = m_new
        yield
        acc_scr[...] = alpha * acc_scr[...] + _mm(p.astype(BF16), cb)

    def new_mask(n_r, n_k, div, inclusive):
        row = lax.broadcasted_iota(jnp.int32, (n_r, n_k), 0)
        col = lax.broadcasted_iota(jnp.int32, (n_r, n_k), 1)
        qpos = row // div if inclusive else row % div
        return (col <= qpos) if inclusive else (col < qpos)

    @pl.when(j == 0)
    def _():
        wst_scr[0:MLA_HEADS * MLA_NOPE, :] = wukt_ref[...]
        wst_scr[MLA_HEADS * MLA_NOPE:, :] = qabs_ref[...]
        m_scr[...] = jnp.full(m_scr.shape, NEG_INF, F32)
        l_scr[...] = jnp.zeros(l_scr.shape, F32)
        acc_scr[...] = jnp.zeros(acc_scr.shape, F32)
        tt = tt_ref[...]
        qbd = qbd_ref[...]
        g_new, r_new = _sample_sb_tile(qbd, tt, [nk_ref[...]], [nv_ref[...]],
                                       jnp.zeros((n_sb, LANES), F32), new_mask(n_sb, page, dec_seq, False), n_sb)
        _staggered([mla_tile([nlat_ref[...]], [nkr_ref[...]], new_mask), g_new], offset=0)
        g_head, r_head = _sample_sb_tile(qbd, tt, [r[...].reshape(kd, page) for r in k_refs],
                                         [r[...].reshape(kd, page) for r in v_refs], r_new[0], None, n_sb)
        for _ in g_head:
            pass
        ocarry_ref[...] = r_head[0]
        osb_ref[...] = r_new[1] + r_head[1]

    for _ in mla_tile([r[...] for r in lat_refs], [r[...] for r in kr_refs], None):
        pass

    @pl.when(j == pl.num_programs(1) - 1)
    def _():
        olat_ref[...] = acc_scr[...] / l_scr[...]


def _sample_tail_kernel(*refs, n_pg, dec_seq, page):
    pt_ref, alive_ref = refs[0:2]
    del pt_ref
    qbd_ref, tt_ref, acc_in_ref, carry_in_ref = refs[2:6]
    k_refs = refs[6:6 + n_pg]
    v_refs = refs[6 + n_pg:6 + 2 * n_pg]
    o_ref = refs[6 + 2 * n_pg]
    carry_scr, acc_scr = refs[7 + 2 * n_pg:]
    b = pl.program_id(0)
    j = pl.program_id(1)
    n_sb = SB_HEADS * dec_seq
    kd = SB_KV_HEADS * SB_HEAD_DIM

    @pl.when(j == 0)
    def _():
        carry_scr[...] = carry_in_ref[...]
        acc_scr[...] = acc_in_ref[...]

    @pl.when(alive_ref[b] > 0)
    def _():
        gen, res = _sample_sb_tile(qbd_ref[...], tt_ref[...], [r[...].reshape(kd, page) for r in k_refs],
                                   [r[...].reshape(kd, page) for r in v_refs], carry_scr[...], None, n_sb)
        for _ in gen:
            pass
        carry_scr[...] = res[0]
        acc_scr[...] += res[1]

    @pl.when(j == pl.num_programs(1) - 1)
    def _():
        o_ref[...] = acc_scr[...]


def _sample_call(layer, page_table, qabs, qr, qbd, wukt, tt, nlat, nkr, nk, nv,
                 cache_lat, cache_krt, cache_kt, cache_vt, n_pg, n_head):
    dec_batch, n_pages = page_table.shape
    page = cache_lat.shape[2]
    dec_seq = qabs.shape[1] // MLA_HEADS
    n_steps = n_pages // n_pg
    n_rows = MLA_HEADS * dec_seq
    n_sb = SB_HEADS * dec_seq
    kd = SB_KV_HEADS * SB_HEAD_DIM

    def per_seq(a):
        return pl.BlockSpec((None,) + a.shape[1:], lambda b, j, pt: (b,) + (0,) * (a.ndim - 1))

    def const(a):
        return pl.BlockSpec(a.shape, lambda b, j, pt: (0,) * a.ndim)

    def paged(a, g):
        nd = a.ndim - 2
        return pl.BlockSpec(
            (None, None) + a.shape[2:],
            lambda b, j, pt: (layer, pt[b, (n_steps - 1 - j) * n_pg + g]) + (0,) * nd)

    def head_page(a, g):
        nd = a.ndim - 2
        return pl.BlockSpec(
            (None, None) + a.shape[2:],
            lambda b, j, pt: (layer, pt[b, n_pages - n_head + g]) + (0,) * nd)

    in_specs = [per_seq(qabs), per_seq(qr), per_seq(qbd), const(wukt), const(tt),
                per_seq(nlat), per_seq(nkr), per_seq(nk), per_seq(nv)]
    operands = [qabs, qr, qbd, wukt, tt, nlat, nkr, nk, nv]
    for cache in (cache_lat, cache_krt):
        for g in range(n_pg):
            in_specs.append(paged(cache, g))
            operands.append(cache)
    for cache in (cache_kt, cache_vt):
        for g in range(n_head):
            in_specs.append(head_page(cache, g))
            operands.append(cache)

    return pl.pallas_call(
        functools.partial(_sample_kernel, n_pg=n_pg, n_head=n_head, dec_seq=dec_seq, page=page),
        grid_spec=pltpu.PrefetchScalarGridSpec(
            num_scalar_prefetch=1,
            grid=(dec_batch, n_steps),
            in_specs=in_specs,
            out_specs=(
                pl.BlockSpec((None, n_rows, KV_LORA), lambda b, j, pt: (b, 0, 0)),
                pl.BlockSpec((None, kd, n_sb), lambda b, j, pt: (b, 0, 0)),
                pl.BlockSpec((None, n_sb, LANES), lambda b, j, pt: (b, 0, 0)),
            ),
            scratch_shapes=[
                pltpu.VMEM((MLA_HEADS * MLA_NOPE + n_rows, KV_LORA), BF16),
                pltpu.VMEM((n_rows, LANES), F32),
                pltpu.VMEM((n_rows, LANES), F32),
                pltpu.VMEM((n_rows, KV_LORA), F32),
            ],
        ),
        out_shape=(
            jax.ShapeDtypeStruct((dec_batch, n_rows, KV_LORA), F32),
            jax.ShapeDtypeStruct((dec_batch, kd, n_sb), F32),
            jax.ShapeDtypeStruct((dec_batch, n_sb, LANES), F32),
        ),
        compiler_params=_cparams(("parallel", "arbitrary")),
        name="sample_attn",
    )(page_table, *operands)


def _sample_tail_call(layer, page_table, alive, qbd, tt, acc_in, carry_in, cache_kt, cache_vt, n_tail, n_pg):
    dec_batch = page_table.shape[0]
    page = cache_kt.shape[-1]
    n_sb = qbd.shape[1]
    dec_seq = n_sb // SB_HEADS
    n_steps = n_tail // n_pg
    kd = SB_KV_HEADS * SB_HEAD_DIM

    def per_seq(a):
        return pl.BlockSpec((None,) + a.shape[1:], lambda b, j, pt, al: (b,) + (0,) * (a.ndim - 1))

    def paged(a, g):
        nd = a.ndim - 2
        return pl.BlockSpec(
            (None, None) + a.shape[2:],
            lambda b, j, pt, al: (layer, jnp.where(al[b] > 0, pt[b, (n_steps - 1 - j) * n_pg + g], 0)) + (0,) * nd)

    in_specs = [per_seq(qbd), pl.BlockSpec(tt.shape, lambda b, j, pt, al: (0, 0)), per_seq(acc_in), per_seq(carry_in)]
    operands = [qbd, tt, acc_in, carry_in]
    for cache in (cache_kt, cache_vt):
        for g in range(n_pg):
            in_specs.append(paged(cache, g))
            operands.append(cache)

    return pl.pallas_call(
        functools.partial(_sample_tail_kernel, n_pg=n_pg, dec_seq=dec_seq, page=page),
        grid_spec=pltpu.PrefetchScalarGridSpec(
            num_scalar_prefetch=2,
            grid=(dec_batch, n_steps),
            in_specs=in_specs,
            out_specs=pl.BlockSpec((None, kd, n_sb), lambda b, j, pt, al: (b, 0, 0)),
            scratch_shapes=[pltpu.VMEM((n_sb, LANES), F32), pltpu.VMEM((kd, n_sb), F32)],
        ),
        out_shape=jax.ShapeDtypeStruct((dec_batch, kd, n_sb), F32),
        compiler_params=_cparams(("parallel", "arbitrary")),
        name="sample_sb_tail",
    )(page_table, alive, *operands)


def _uv_kernel(x_ref, w_ref, o_ref):
    o_ref[...] = _mm(x_ref[...].astype(BF16), w_ref[...]).astype(BF16)


def _uv_call(o_lat, wuv_bd):
    n = o_lat.shape[0]
    return pl.pallas_call(
        _uv_kernel,
        grid=(1,),
        in_specs=[pl.BlockSpec(o_lat.shape, lambda i: (0, 0)), pl.BlockSpec(wuv_bd.shape, lambda i: (0, 0))],
        out_specs=pl.BlockSpec((n, MLA_HEADS * MLA_V), lambda i: (0, 0)),
        out_shape=jax.ShapeDtypeStruct((n, MLA_HEADS * MLA_V), BF16),
        compiler_params=_cparams(("arbitrary",)),
        name="sample_uv",
    )(o_lat, wuv_bd)


def _merge_kernel(*refs, n_prompt_tiles, with_router):
    if with_router:
        (oap_ref, obp_ref, oas_ref, obs_ref, g_ref, h_ref, wa_ref, wb_ref, wo_ref, gf_ref, wr_ref,
         hn_ref, xn_ref, ids_ref, wts_ref) = refs
    else:
        (oap_ref, obp_ref, oas_ref, obs_ref, g_ref, h_ref, wa_ref, wb_ref, wo_ref, gf_ref,
         hn_ref, xn_ref) = refs
    is_prompt = pl.program_id(0) < n_prompt_tiles
    oa = jnp.where(is_prompt, oap_ref[...], oas_ref[...])
    ob = jnp.where(is_prompt, obp_ref[...], obs_ref[...])
    br_a = _mm(oa, wa_ref[...])
    br_b = _mm(ob, wb_ref[...])
    mixed = g_ref[:, 0:D_MODEL].astype(F32) * br_a + g_ref[:, D_MODEL:2 * D_MODEL].astype(F32) * br_b
    hn = h_ref[...] + _mm(mixed.astype(BF16), wo_ref[...])
    hn_ref[...] = hn
    xn = hn * lax.rsqrt(jnp.mean(hn * hn, axis=-1, keepdims=True) + EPS) * gf_ref[...]
    xn_ref[...] = xn.astype(xn_ref.dtype)
    if with_router:
        logits = jnp.dot(xn, wr_ref[...], preferred_element_type=F32, precision=lax.Precision.HIGHEST)
        lane = lax.broadcasted_iota(jnp.int32, logits.shape, 1)
        valid = lane < N_EXPERTS
        lg = jnp.where(valid, logits, NEG_INF)
        m1 = jnp.max(lg, axis=1, keepdims=True)
        i1 = jnp.min(jnp.where(lg == m1, lane, LANES), axis=1, keepdims=True)
        lg2 = jnp.where(lane == i1, NEG_INF, lg)
        m2 = jnp.max(lg2, axis=1, keepdims=True)
        i2 = jnp.min(jnp.where(lg2 == m2, lane, LANES), axis=1, keepdims=True)
        e2 = jnp.exp(m2 - m1)
        w1 = 1.0 / (1.0 + e2)
        w2 = e2 / (1.0 + e2)
        ids_ref[...] = jnp.where(lane == 0, i1, jnp.where(lane == 1, i2, 0))
        wts_ref[...] = jnp.where(lane == 0, w1, jnp.where(lane == 1, w2, 0.0))


def _merge_call(oa_p, ob_p, oa_s, ob_s, gates, h, wa, wb, wo, gf, wr, tm):
    T = h.shape[0]
    n_p = oa_p.shape[0] // tm
    with_router = wr is not None
    row = lambda w: pl.BlockSpec((tm, w), lambda i: (i, 0))
    full = lambda a: pl.BlockSpec(a.shape, lambda i: (0, 0))
    p_spec = pl.BlockSpec((tm, oa_p.shape[1]), lambda i: (jnp.minimum(i, n_p - 1), 0))
    s_spec = pl.BlockSpec((tm, oa_s.shape[1]), lambda i: (jnp.maximum(i - n_p, 0), 0))
    in_specs = [p_spec, p_spec, s_spec, s_spec, row(2 * D_MODEL), row(D_MODEL), full(wa), full(wb), full(wo), full(gf)]
    operands = [oa_p, ob_p, oa_s, ob_s, gates, h, wa, wb, wo, gf]
    out_shape = [jax.ShapeDtypeStruct((T, D_MODEL), F32),
                 jax.ShapeDtypeStruct((T, D_MODEL), F32 if with_router else BF16)]
    out_specs = [row(D_MODEL), row(D_MODEL)]
    if with_router:
        in_specs.append(full(wr))
        operands.append(wr)
        out_shape += [jax.ShapeDtypeStruct((T, LANES), jnp.int32), jax.ShapeDtypeStruct((T, LANES), F32)]
        out_specs += [row(LANES), row(LANES)]
    return pl.pallas_call(
        functools.partial(_merge_kernel, n_prompt_tiles=n_p, with_router=with_router),
        grid=(T // tm,),
        in_specs=in_specs,
        out_specs=tuple(out_specs),
        out_shape=tuple(out_shape),
        compiler_params=_cparams(("parallel",)),
        name="merge",
    )(*operands)


def _swiglu_tile(x, wg, wu):
    a = _mm(x, wg)
    return (a * (1.0 / (1.0 + jnp.exp(-a))) * _mm(x, wu)).astype(BF16)


def _ffn_kernel(x_ref, h_ref, wg_ref, wu_ref, wd_ref, o_ref, acc_scr):
    f = pl.program_id(1)

    @pl.when(f == 0)
    def _():
        acc_scr[...] = h_ref[...]

    acc_scr[...] += _mm(_swiglu_tile(x_ref[...], wg_ref[...], wu_ref[...]), wd_ref[...])

    @pl.when(f == pl.num_programs(1) - 1)
    def _():
        o_ref[...] = acc_scr[...]


def _ffn_call(xn, h, wg, wu, wd, tm, tf):
    T = h.shape[0]
    d_ff = wg.shape[1]
    return pl.pallas_call(
        _ffn_kernel,
        grid=(T // tm, d_ff // tf),
        in_specs=[
            pl.BlockSpec((tm, D_MODEL), lambda i, f: (i, 0)),
            pl.BlockSpec((tm, D_MODEL), lambda i, f: (i, 0)),
            pl.BlockSpec((D_MODEL, tf), lambda i, f: (0, f)),
            pl.BlockSpec((D_MODEL, tf), lambda i, f: (0, f)),
            pl.BlockSpec((tf, D_MODEL), lambda i, f: (f, 0)),
        ],
        out_specs=pl.BlockSpec((tm, D_MODEL), lambda i, f: (i, 0)),
        out_shape=jax.ShapeDtypeStruct((T, D_MODEL), F32),
        scratch_shapes=[pltpu.VMEM((tm, D_MODEL), F32)],
        compiler_params=_cparams(("parallel", "arbitrary")),
        name="ffn_dense",
    )(xn, h, wg, wu, wd)


MOE_TILE = 1024


def _row_copy(src_hbm, row, dst, r, sem):
    return pltpu.make_async_copy(src_hbm.at[pl.ds(row, 1)], dst.at[pl.ds(r, 1)], sem)


def _gather_kernel(idx_ref, src_hbm, o_ref, sem, *, rows):
    def start(r, c):
        _row_copy(src_hbm, idx_ref[0, r], o_ref, r, sem).start()
        return c

    lax.fori_loop(0, rows, start, 0, unroll=8)

    def wait(r, c):
        _row_copy(src_hbm, 0, o_ref, r, sem).wait()
        return c

    lax.fori_loop(0, rows, wait, 0, unroll=8)


def _gather_call(src, idx, rows):
    n_tiles = idx.shape[0]
    d = src.shape[1]
    return pl.pallas_call(
        functools.partial(_gather_kernel, rows=rows),
        grid=(n_tiles,),
        in_specs=[
            pl.BlockSpec((None, 1, rows), lambda i: (i, 0, 0), memory_space=pltpu.SMEM),
            pl.BlockSpec(memory_space=pl.ANY),
        ],
        out_specs=pl.BlockSpec((rows, d), lambda i: (i, 0)),
        out_shape=jax.ShapeDtypeStruct((n_tiles * rows, d), src.dtype),
        scratch_shapes=[pltpu.SemaphoreType.DMA(())],
        compiler_params=_cparams(("arbitrary",)),
        name="moe_gather",
    )(idx, src)


def _gmm_kernel(te_ref, tv_ref, x_ref, wg_ref, wu_ref, wd_ref, o_ref, xb_scr, acc_scr):
    del te_ref
    i = pl.program_id(0)
    f = pl.program_id(1)
    used = tv_ref[i] > 0

    @pl.when(used & (f == 0))
    def _():
        xb_scr[...] = x_ref[...].astype(BF16)

    @pl.when(used)
    def _():
        y = _mm(_swiglu_tile(xb_scr[...], wg_ref[...], wu_ref[...]), wd_ref[...])

        @pl.when(f == 0)
        def _():
            acc_scr[...] = y

        @pl.when(f > 0)
        def _():
            acc_scr[...] += y

    @pl.when(f == pl.num_programs(1) - 1)
    def _():
        @pl.when(used)
        def _():
            o_ref[...] = acc_scr[...]

        @pl.when(jnp.logical_not(used))
        def _():
            o_ref[...] = jnp.zeros(o_ref.shape, F32)


def _gmm_call(tile_expert, tile_used, x_sorted, wg, wu, wd, layer, rows, tf):
    n_tiles = x_sorted.shape[0] // rows
    d_ff = wg.shape[3]
    n_f = d_ff // tf
    w_in = lambda i, f, te, tv: (layer, te[i], 0, jnp.where(tv[i] > 0, f, n_f - 1))
    w_out = lambda i, f, te, tv: (layer, te[i], jnp.where(tv[i] > 0, f, n_f - 1), 0)
    return pl.pallas_call(
        _gmm_kernel,
        grid_spec=pltpu.PrefetchScalarGridSpec(
            num_scalar_prefetch=2,
            grid=(n_tiles, n_f),
            in_specs=[
                pl.BlockSpec((rows, D_MODEL), lambda i, f, te, tv: (i, 0)),
                pl.BlockSpec((None, None, D_MODEL, tf), w_in),
                pl.BlockSpec((None, None, D_MODEL, tf), w_in),
                pl.BlockSpec((None, None, tf, D_MODEL), w_out),
            ],
            out_specs=pl.BlockSpec((rows, D_MODEL), lambda i, f, te, tv: (i, 0)),
            scratch_shapes=[pltpu.VMEM((rows, D_MODEL), BF16), pltpu.VMEM((rows, D_MODEL), F32)],
        ),
        out_shape=jax.ShapeDtypeStruct(x_sorted.shape, F32),
        compiler_params=_cparams(("parallel", "arbitrary")),
        name="ffn_experts",
    )(tile_expert, tile_used, x_sorted, wg, wu, wd)


def _combine_kernel(pos_ref, h_ref, w_ref, y_hbm, o_ref, buf, sem, *, rows):
    def start(r, c):
        _row_copy(y_hbm, pos_ref[0, r], buf, r, sem).start()
        return c

    lax.fori_loop(0, 2 * rows, start, 0, unroll=8)

    def wait(r, c):
        _row_copy(y_hbm, 0, buf, r, sem).wait()
        return c

    lax.fori_loop(0, 2 * rows, wait, 0, unroll=8)
    w = w_ref[...]
    o_ref[...] = h_ref[...] + (w[:, 0:1] * buf[0:rows, :] + w[:, 1:2] * buf[rows:2 * rows, :])


def _combine_call(pos, h, wts, y_sorted, rows):
    T = h.shape[0]
    return pl.pallas_call(
        functools.partial(_combine_kernel, rows=rows),
        grid=(T // rows,),
        in_specs=[
            pl.BlockSpec((None, 1, 2 * rows), lambda i: (i, 0, 0), memory_space=pltpu.SMEM),
            pl.BlockSpec((rows, D_MODEL), lambda i: (i, 0)),
            pl.BlockSpec((rows, LANES), lambda i: (i, 0)),
            pl.BlockSpec(memory_space=pl.ANY),
        ],
        out_specs=pl.BlockSpec((rows, D_MODEL), lambda i: (i, 0)),
        out_shape=jax.ShapeDtypeStruct((T, D_MODEL), F32),
        scratch_shapes=[pltpu.VMEM((2 * rows, D_MODEL), F32), pltpu.SemaphoreType.DMA(())],
        compiler_params=_cparams(("arbitrary",)),
        name="moe_combine",
    )(pos, h, wts, y_sorted)


def _route_plan(ids, rows):
    T = ids.shape[0]
    n_slots = 2 * T
    n_tiles = (n_slots + N_EXPERTS * (rows - 1) + rows - 1) // rows
    e_slot = ids.reshape(n_slots)
    onehot = (e_slot[:, None] == jnp.arange(N_EXPERTS, dtype=jnp.int32)[None, :]).astype(jnp.int32)
    rank = jnp.sum((jnp.cumsum(onehot, axis=0) - onehot) * onehot, axis=1)
    counts = jnp.sum(onehot, axis=0)
    padded = ((counts + rows - 1) // rows) * rows
    ends = jnp.cumsum(padded)
    starts = ends - padded
    dest = starts[e_slot] + rank
    src = jnp.zeros((n_tiles * rows,), jnp.int32).at[dest].set(jnp.arange(n_slots, dtype=jnp.int32) // 2)
    tile_start = jnp.arange(n_tiles, dtype=jnp.int32) * rows
    tile_expert = jnp.minimum(jnp.sum((tile_start[:, None] >= ends[None, :]).astype(jnp.int32), axis=1),
                              N_EXPERTS - 1)
    tile_used = (tile_start < ends[-1]).astype(jnp.int32)
    return src.reshape(n_tiles, 1, rows), tile_expert, tile_used, dest.reshape(T, 2)


def _moe_call(xn32, h, ids, wts, wg, wu, wd, layer, tm, tf):
    T = h.shape[0]
    src, tile_expert, tile_used, dest = _route_plan(ids[:, 0:2], MOE_TILE)
    x_sorted = _gather_call(xn32, src, MOE_TILE)
    y_sorted = _gmm_call(tile_expert, tile_used, x_sorted, wg, wu, wd, layer, MOE_TILE, tf)
    pos = dest.reshape(T // tm, tm, 2).transpose(0, 2, 1).reshape(T // tm, 1, 2 * tm)
    return _combine_call(pos, h, wts, y_sorted, tm)


def _head_block_cols(w_rope, w_nope):
    pad = jnp.zeros(w_rope.shape[:-1] + (LANES - MLA_ROPE - MLA_NOPE,), w_rope.dtype)
    blk = jnp.concatenate([w_rope, w_nope, pad], axis=-1)
    return blk.reshape(blk.shape[:-2] + (blk.shape[-2] * LANES,))


def _frame(rope=None, nope=None):
    v = jnp.zeros((LANES,), F32)
    if rope is not None:
        v = v.at[0:MLA_ROPE].set(rope)
    if nope is not None:
        v = v.at[NOPE_LO:NOPE_HI].set(nope)
    return v


def _row(v):
    return jnp.zeros((D_MODEL,), F32).at[0:v.shape[0]].set(v)


def _rope_table(pos):
    inv_freq = ROPE_THETA ** (-jnp.arange(HALF_ROPE, dtype=F32) / HALF_ROPE)
    ang = pos.astype(F32)[:, None] * inv_freq[None, :]
    cos, sin = jnp.cos(ang), jnp.sin(ang)
    n = pos.shape[0]
    z16 = jnp.zeros((n, HALF_ROPE), F32)
    cos_t = jnp.concatenate([cos, cos, jnp.ones((n, LANES - MLA_ROPE), F32)], axis=1)
    sina = jnp.concatenate([-sin, jnp.zeros((n, LANES - HALF_ROPE), F32)], axis=1)
    sinb = jnp.concatenate([z16, sin, jnp.zeros((n, LANES - MLA_ROPE), F32)], axis=1)
    return jnp.concatenate([cos_t, sina, sinb], axis=1)


def kernel(x_prompt, x_sample, cache_mla_latent, cache_mla_krope, cache_sb_k, cache_sb_v, page_table, g_attn_norm, w_in, g_q_lat, w_q_up, g_q_nope, g_q_rope, g_kv_lat, g_k_rope, w_uk, g_k_nope, w_uv, w_mla_out, w_sb_out, w_o, g_ffn_norm, w_ffn_gate, w_ffn_up, w_ffn_down, w_router, w_exp_gate, w_exp_up, w_exp_down):
    batch, seq, _ = x_prompt.shape
    dec_batch, dec_seq, _ = x_sample.shape
    depth = w_in.shape[0]
    n_pages = page_table.shape[1]
    page = cache_mla_latent.shape[2]
    past_len = n_pages * page
    t_p = batch * seq
    t_s = dec_batch * dec_seq
    T = t_p + t_s
    assert page == LANES and dec_seq <= page

    tm = _pick(math.gcd(t_p, t_s), (512, 256, 128, 64, 32, 16))
    tq = _pick(seq, (512, 256, 128))
    ck = min(SB_CHUNK, tq)
    n_pg = _pick(n_pages, (16, 8, 4, 2, 1))
    n_head = min(4, n_pages)
    n_tail = n_pages - n_head
    n_pg_tail = _pick(n_tail, (15, 12, 10, 8, 6, 5, 4, 3, 2, 1)) if n_tail else 0
    tm_ffn = _pick(T, (768, 512, 384, 256, 192, 128, 64, 32, 16))

    pos = jnp.concatenate([
        jnp.tile(jnp.arange(seq, dtype=jnp.int32), batch),
        jnp.tile(past_len + jnp.arange(dec_seq, dtype=jnp.int32), dec_batch)])
    tab = _rope_table(pos)
    tt_prompt = _suffix_matrix(ck)
    tt_page = _suffix_matrix(page)

    cache_krt = jnp.transpose(cache_mla_krope, (0, 1, 3, 2))
    cache_kt = jnp.transpose(cache_sb_k, (0, 1, 3, 4, 2))
    cache_vt = jnp.transpose(cache_sb_v, (0, 1, 3, 4, 2))

    w_exp_gate_bf = w_exp_gate.astype(BF16)
    w_exp_up_bf = w_exp_up.astype(BF16)
    w_exp_down_bf = w_exp_down.astype(BF16)

    h = jnp.concatenate([x_prompt.reshape(t_p, D_MODEL), x_sample.reshape(t_s, D_MODEL)], axis=0)
    outs = [[] for _ in range(8)]
    sizes = np.cumsum([Q_LORA, KV_LORA, MLA_ROPE, SB_HEADS * SB_HEAD_DIM, SB_KV_HEADS * SB_HEAD_DIM,
                       SB_KV_HEADS * SB_HEAD_DIM, D_MODEL])

    for l in range(depth):
        w_ql, w_c, w_kr, w_sq, w_sk, w_sv, w_ga, w_gb = jnp.split(w_in[l], sizes, axis=1)
        w_kr_pad = jnp.concatenate([w_kr, jnp.zeros((D_MODEL, LANES - MLA_ROPE), F32)], axis=1)
        w_all = jnp.concatenate([w_ql, w_c, w_kr_pad, w_sq * SB_SCALE, w_sk, w_sv, w_ga, w_gb], axis=1).astype(BF16)
        wq3 = w_q_up[l].reshape(Q_LORA, MLA_HEADS, MLA_NOPE + MLA_ROPE)
        wqu = _head_block_cols(wq3[..., MLA_NOPE:], wq3[..., :MLA_NOPE]).astype(BF16)
        wuk = _head_block_cols(jnp.zeros((KV_LORA, MLA_HEADS, MLA_ROPE), F32), w_uk[l]).astype(BF16)
        gv = jnp.stack([
            g_attn_norm[l], _row(g_q_lat[l]),
            _row(_frame(rope=g_q_rope[l] * MLA_SCALE)), _row(_frame(nope=g_q_nope[l] * MLA_SCALE)),
            _row(g_kv_lat[l]), _row(_frame(rope=g_k_rope[l])), _row(_frame(nope=g_k_nope[l])),
            jnp.zeros((D_MODEL,), F32)])
        wuv_bd = jnp.zeros((MLA_HEADS * KV_LORA, MLA_HEADS * MLA_V), F32)
        for hh in range(MLA_HEADS):
            wuv_bd = wuv_bd.at[hh * KV_LORA:(hh + 1) * KV_LORA, hh * MLA_V:(hh + 1) * MLA_V].set(w_uv[l][:, hh, :])
        wuv_bd = wuv_bd.astype(BF16)

        (qcat, kcat, c, cbf, kr, sq, sk, sv, skbf, svbf, gates) = _proj_call(h, tab, gv, w_all, wqu, wuk, tm)

        oa_p = _mla_call(qcat, kcat, cbf, wuv_bd, batch, seq, tq)
        k_heads = skbf[:t_p].reshape(batch, seq, SB_KV_HEADS, SB_HEAD_DIM).transpose(0, 2, 1, 3)
        v_heads = svbf[:t_p].reshape(batch, seq, SB_KV_HEADS, SB_HEAD_DIM).transpose(0, 2, 1, 3)
        v_heads = jnp.concatenate([v_heads, v_heads], axis=-1)
        ob_p = _sb_call(sq, k_heads, v_heads, tt_prompt, batch, seq, tq, ck)

        q_s = qcat[t_p:]
        wabs = jnp.zeros((MLA_HEADS, LANES, KV_LORA), F32)
        wabs = wabs.at[:, NOPE_LO:NOPE_HI, :].set(
            jnp.transpose(w_uk[l] * g_k_nope[l][None, None, :], (1, 2, 0)))
        qabs = _absorb_call(q_s, wabs.astype(BF16))
        qabs = qabs.reshape(dec_batch, dec_seq * MLA_HEADS, KV_LORA)
        qr = q_s.reshape(dec_batch, dec_seq * MLA_HEADS, LANES)[:, :, 0:MLA_ROPE]
        sq_s = sq[t_p:].reshape(dec_batch, dec_seq, SB_HEADS, SB_HEAD_DIM).transpose(0, 2, 1, 3)
        sq_s = sq_s.reshape(dec_batch, SB_KV_HEADS, SB_GROUP * dec_seq, SB_HEAD_DIM)
        eye = jnp.eye(SB_KV_HEADS, dtype=BF16)
        qbd = (sq_s[:, :, :, None, :] * eye[None, :, None, :, None]).reshape(
            dec_batch, SB_HEADS * dec_seq, SB_KV_HEADS * SB_HEAD_DIM)
        wukt = jnp.transpose(w_uk[l], (1, 2, 0)).reshape(MLA_HEADS * MLA_NOPE, KV_LORA).astype(BF16)
        pad_k = page - dec_seq
        nlat = jnp.pad(c[t_p:].reshape(dec_batch, dec_seq, KV_LORA), ((0, 0), (0, pad_k), (0, 0)))
        nkr = jnp.pad(kr[t_p:].reshape(dec_batch, dec_seq, MLA_ROPE).transpose(0, 2, 1), ((0, 0), (0, 0), (0, pad_k)))
        nk = jnp.pad(sk[t_p:].reshape(dec_batch, dec_seq, -1).transpose(0, 2, 1), ((0, 0), (0, 0), (0, pad_k)))
        nv = jnp.pad(sv[t_p:].reshape(dec_batch, dec_seq, -1).transpose(0, 2, 1), ((0, 0), (0, 0), (0, pad_k)))
        o_lat_s, o_sbt, sb_carry = _sample_call(l, page_table, qabs, qr, qbd, wukt, tt_page, nlat, nkr, nk, nv,
                                                cache_mla_latent, cache_krt, cache_kt, cache_vt, n_pg, n_head)
        if n_tail > 0:
            alive = (jnp.max(sb_carry, axis=(1, 2)) >= SB_DEAD_LOG).astype(jnp.int32)
            o_sbt = lax.cond(
                jnp.max(alive) > 0,
                functools.partial(_sample_tail_call, l, page_table, alive, qbd, tt_page, o_sbt, sb_carry,
                                  cache_kt, cache_vt, n_tail, n_pg_tail),
                lambda: o_sbt)
        o_lat_s = o_lat_s.reshape(dec_batch, dec_seq, MLA_HEADS, KV_LORA).reshape(t_s, MLA_HEADS * KV_LORA)
        oa_s = _uv_call(o_lat_s, wuv_bd)
        o5 = o_sbt.reshape(dec_batch, SB_KV_HEADS, SB_HEAD_DIM, SB_KV_HEADS, SB_GROUP, dec_seq)
        o5 = jnp.stack([o5[:, kv, :, kv] for kv in range(SB_KV_HEADS)], axis=1)
        ob_s = o5.transpose(0, 4, 1, 3, 2).reshape(t_s, SB_HEADS * SB_HEAD_DIM).astype(BF16)

        j = l // 2
        moe = l % 2 == 1
        wr = None
        if moe:
            wr = jnp.concatenate([w_router[j], jnp.zeros((D_MODEL, LANES - N_EXPERTS), F32)], axis=1)
        res = _merge_call(oa_p, ob_p, oa_s, ob_s, gates, h,
                          w_mla_out[l].astype(BF16), w_sb_out[l].astype(BF16), w_o[l].astype(BF16),
                          g_ffn_norm[l][None, :], wr, tm)
        if moe:
            hn, xn32, route_ids, route_wts = res
            h = _moe_call(xn32, hn, route_ids, route_wts, w_exp_gate_bf, w_exp_up_bf, w_exp_down_bf, j,
                          _pick(T, (256, 128, 64, 32, 16, 8)), 512)
        else:
            hn, xn = res
            d_ff = w_ffn_gate.shape[2]
            h = _ffn_call(xn, hn, w_ffn_gate[j].astype(BF16), w_ffn_up[j].astype(BF16),
                          w_ffn_down[j].astype(BF16), tm_ffn, _pick(d_ff, (1408, 512, 256, 128)))

        for idx, a in enumerate((c, kr, sk, sv)):
            outs[idx].append(a[:t_p])
            outs[4 + idx].append(a[t_p:])

    def stack_p(xs, tail):
        return jnp.stack(xs).reshape((depth, batch, seq) + tail)

    def stack_s(xs, tail):
        return jnp.stack(xs).reshape((depth, dec_batch, dec_seq) + tail)

    kv_tail = (SB_KV_HEADS, SB_HEAD_DIM)
    return (h[:t_p].reshape(batch, seq, D_MODEL), h[t_p:].reshape(dec_batch, dec_seq, D_MODEL),
            stack_p(outs[0], (KV_LORA,)), stack_p(outs[1], (MLA_ROPE,)),
            stack_p(outs[2], kv_tail), stack_p(outs[3], kv_tail),
            stack_s(outs[4], (KV_LORA,)), stack_s(outs[5], (MLA_ROPE,)),
            stack_s(outs[6], kv_tail), stack_s(outs[7], kv_tail))
```

```python
import functools
import math

import jax
import jax.numpy as jnp
import numpy as np
from jax import lax
from jax.experimental import pallas as pl
from jax.experimental.pallas import tpu as pltpu

D_MODEL = 1024
MLA_HEADS = 8
MLA_NOPE = 64
MLA_ROPE = 32
MLA_V = 64
Q_LORA = 256
KV_LORA = 128
ROPE_THETA = 10000.0
MLA_SCALE = (MLA_NOPE + MLA_ROPE) ** -0.5
SB_HEADS = 8
SB_KV_HEADS = 4
SB_GROUP = SB_HEADS // SB_KV_HEADS
SB_HEAD_DIM = 64
SB_SCALE = SB_HEAD_DIM ** -0.5
N_EXPERTS = 8
EPS = 1e-6
NEG_INF = -1e30

LANES = 128
HALF_ROPE = MLA_ROPE // 2
NOPE_LO, NOPE_HI = MLA_ROPE, MLA_ROPE + MLA_NOPE
VMEM_LIMIT = 56 * 1024 * 1024

F32 = jnp.float32
BF16 = jnp.bfloat16


def _pick(n, candidates):
    for c in candidates:
        if n % c == 0:
            return c
    return n


def _mm(a, b):
    return jnp.dot(a, b, preferred_element_type=F32)


def _mm_nt(a, b):
    return lax.dot_general(a, b, (((1,), (1,)), ((), ())), preferred_element_type=F32)


def _cparams(sem):
    return pltpu.CompilerParams(dimension_semantics=sem, vmem_limit_bytes=VMEM_LIMIT)


def _staggered(gens, offset=1):
    pending = list(gens)
    active = []
    rnd = 0
    while pending or active:
        if pending and (offset == 0 or rnd % offset == 0):
            if offset == 0:
                active.extend(pending)
                pending = []
            else:
                active.append(pending.pop(0))
        for g in list(active):
            try:
                next(g)
            except StopIteration:
                active.remove(g)
        rnd += 1


_C_QLAT, _C_CKV, _C_KR, _C_SQ, _C_SK, _C_SV, _C_G, _C_END = 0, 256, 384, 512, 1024, 1280, 1536, 3584


def _rope_rotate(y, cos, sina, sinb):
    return y * cos + pltpu.roll(y, LANES - HALF_ROPE, 1) * sina + pltpu.roll(y, HALF_ROPE, 1) * sinb


def _proj_kernel(h_ref, tab_ref, gv_ref, w_ref, wqu_ref, wuk_ref,
                 qcat_ref, kcat_ref, c_ref, cbf_ref, kr_ref, sq_ref, sk_ref, sv_ref,
                 skbf_ref, svbf_ref, gates_ref):
    x = h_ref[...]
    g_attn = gv_ref[0:1, :]
    xn = (x * lax.rsqrt(jnp.mean(x * x, axis=-1, keepdims=True) + EPS) * g_attn).astype(BF16)

    cos = tab_ref[:, 0:LANES]
    sina = tab_ref[:, LANES:2 * LANES]
    sinb = tab_ref[:, 2 * LANES:3 * LANES]
    lane = lax.broadcasted_iota(jnp.int32, (1, LANES), 1)
    m_rope = (lane < MLA_ROPE).astype(F32)
    m_nope = ((lane >= NOPE_LO) & (lane < NOPE_HI)).astype(F32)

    q_lat = _mm(xn, w_ref[:, _C_QLAT:_C_CKV])
    g_ql = gv_ref[1:2, 0:Q_LORA]
    ql = (q_lat * lax.rsqrt(jnp.mean(q_lat * q_lat, axis=-1, keepdims=True) + EPS) * g_ql).astype(BF16)
    q = _mm(ql, wqu_ref[...])
    g_qr = gv_ref[2:3, 0:LANES]
    g_qn = gv_ref[3:4, 0:LANES]
    for h in range(MLA_HEADS):
        xh = q[:, h * LANES:(h + 1) * LANES]
        x2 = xh * xh
        ss_r = jnp.sum(x2 * m_rope, axis=-1, keepdims=True)
        ss_n = jnp.sum(x2 * m_nope, axis=-1, keepdims=True)
        sc = lax.rsqrt(ss_r * (1.0 / MLA_ROPE) + EPS) * g_qr + lax.rsqrt(ss_n * (1.0 / MLA_NOPE) + EPS) * g_qn
        y = _rope_rotate(xh * sc, cos, sina, sinb)
        qcat_ref[:, h * LANES:(h + 1) * LANES] = y.astype(BF16)

    c_raw = _mm(xn, w_ref[:, _C_CKV:_C_KR])
    g_kvl = gv_ref[4:5, 0:KV_LORA]
    c = c_raw * lax.rsqrt(jnp.mean(c_raw * c_raw, axis=-1, keepdims=True) + EPS) * g_kvl
    c_ref[...] = c
    cbf = c.astype(BF16)
    cbf_ref[...] = cbf

    kr_raw = _mm(xn, w_ref[:, _C_KR:_C_SQ])
    g_kr = gv_ref[5:6, 0:LANES]
    ss = jnp.sum(kr_raw * kr_raw, axis=-1, keepdims=True)
    krn = _rope_rotate(kr_raw * (lax.rsqrt(ss * (1.0 / MLA_ROPE) + EPS) * g_kr), cos, sina, sinb)
    kr_ref[...] = krn[:, 0:MLA_ROPE]

    kp = _mm(cbf, wuk_ref[...])
    g_kn = gv_ref[6:7, 0:LANES]
    for h in range(MLA_HEADS):
        kh = kp[:, h * LANES:(h + 1) * LANES]
        ss_n = jnp.sum(kh * kh, axis=-1, keepdims=True)
        kn = kh * (lax.rsqrt(ss_n * (1.0 / MLA_NOPE) + EPS) * g_kn)
        kcat_ref[:, h * LANES:(h + 1) * LANES] = (kn + krn).astype(BF16)

    sq_ref[...] = _mm(xn, w_ref[:, _C_SQ:_C_SK]).astype(BF16)
    sk = _mm(xn, w_ref[:, _C_SK:_C_SV])
    sk_ref[...] = sk
    skbf_ref[...] = sk.astype(BF16)
    sv = _mm(xn, w_ref[:, _C_SV:_C_G])
    sv_ref[...] = sv
    svbf_ref[...] = sv.astype(BF16)

    g = _mm(xn, w_ref[:, _C_G:_C_END])
    gates_ref[...] = (1.0 / (1.0 + jnp.exp(-g))).astype(BF16)


def _proj_call(h, tab, gv, w_all, wqu, wuk, tm):
    T = h.shape[0]
    row = lambda w: pl.BlockSpec((tm, w), lambda i: (i, 0))
    full = lambda a: pl.BlockSpec(a.shape, lambda i: (0, 0))
    out_shapes = (
        jax.ShapeDtypeStruct((T, MLA_HEADS * LANES), BF16),
        jax.ShapeDtypeStruct((T, MLA_HEADS * LANES), BF16),
        jax.ShapeDtypeStruct((T, KV_LORA), F32),
        jax.ShapeDtypeStruct((T, KV_LORA), BF16),
        jax.ShapeDtypeStruct((T, MLA_ROPE), F32),
        jax.ShapeDtypeStruct((T, SB_HEADS * SB_HEAD_DIM), BF16),
        jax.ShapeDtypeStruct((T, SB_KV_HEADS * SB_HEAD_DIM), F32),
        jax.ShapeDtypeStruct((T, SB_KV_HEADS * SB_HEAD_DIM), F32),
        jax.ShapeDtypeStruct((T, SB_KV_HEADS * SB_HEAD_DIM), BF16),
        jax.ShapeDtypeStruct((T, SB_KV_HEADS * SB_HEAD_DIM), BF16),
        jax.ShapeDtypeStruct((T, 2 * D_MODEL), BF16),
    )
    return pl.pallas_call(
        _proj_kernel,
        grid=(T // tm,),
        in_specs=[row(D_MODEL), row(3 * LANES), full(gv), full(w_all), full(wqu), full(wuk)],
        out_specs=tuple(row(s.shape[1]) for s in out_shapes),
        out_shape=out_shapes,
        compiler_params=_cparams(("parallel",)),
        name="proj",
    )(h, tab, gv, w_all, wqu, wuk)


def _mla_kernel(qi_tab, ki_tab, q_ref, k_ref, v_ref, wuv_ref, o_ref, m_scr, acc_scr, *, tq):
    t = pl.program_id(1)
    qi = qi_tab[t]
    ki = ki_tab[t]

    @pl.when(ki == 0)
    def _():
        m_scr[...] = jnp.full(m_scr.shape, NEG_INF, F32)
        acc_scr[...] = jnp.zeros(acc_scr.shape, F32)

    def step(masked):
        v = v_ref[...]
        if masked:
            row = lax.broadcasted_iota(jnp.int32, (tq, tq), 0)
            col = lax.broadcasted_iota(jnp.int32, (tq, tq), 1)
            keep = col <= row

        for h in range(MLA_HEADS):
            s = _mm_nt(q_ref[:, h * LANES:(h + 1) * LANES], k_ref[:, h * LANES:(h + 1) * LANES])
            if masked:
                s = jnp.where(keep, s, NEG_INF)
            m_prev = m_scr[h]
            m_new = jnp.maximum(m_prev, jnp.max(s, axis=1, keepdims=True))
            alpha = jnp.exp(m_prev - m_new)
            p = jnp.exp(s - m_new[:, 0:1])
            acc_scr[h] = jnp.concatenate([alpha, alpha], axis=1) * acc_scr[h] + _mm(p.astype(BF16), v)
            m_scr[h] = m_new

    @pl.when(ki < qi)
    def _():
        step(False)

    @pl.when(ki == qi)
    def _():
        step(True)
        o_lat = jnp.concatenate(
            [(acc_scr[h, :, 0:KV_LORA] / acc_scr[h, :, KV_LORA:2 * KV_LORA]).astype(BF16)
             for h in range(MLA_HEADS)], axis=1)
        o_ref[...] = _mm(o_lat, wuv_ref[...]).astype(BF16)


def _tri_tables(nq, descending):
    qi, ki = [], []
    for i in range(nq):
        ks = range(i, -1, -1) if descending else range(i + 1)
        for j in ks:
            qi.append(i)
            ki.append(j)
    return jnp.asarray(np.array(qi, np.int32)), jnp.asarray(np.array(ki, np.int32))


def _mla_call(qcat, kcat, cbf, wuv_bd, batch, seq, tq):
    nq = seq // tq
    qi_tab, ki_tab = _tri_tables(nq, descending=False)
    n_tri = int(qi_tab.shape[0])
    return pl.pallas_call(
        functools.partial(_mla_kernel, tq=tq),
        grid_spec=pltpu.PrefetchScalarGridSpec(
            num_scalar_prefetch=2,
            grid=(batch, n_tri),
            in_specs=[
                pl.BlockSpec((tq, MLA_HEADS * LANES), lambda b, t, qt, kt: (b * nq + qt[t], 0)),
                pl.BlockSpec((tq, MLA_HEADS * LANES), lambda b, t, qt, kt: (b * nq + kt[t], 0)),
                pl.BlockSpec((tq, 2 * KV_LORA), lambda b, t, qt, kt: (b * nq + kt[t], 0)),
                pl.BlockSpec(wuv_bd.shape, lambda b, t, qt, kt: (0, 0)),
            ],
            out_specs=pl.BlockSpec((tq, MLA_HEADS * MLA_V), lambda b, t, qt, kt: (b * nq + qt[t], 0)),
            scratch_shapes=[
                pltpu.VMEM((MLA_HEADS, tq, LANES), F32),
                pltpu.VMEM((MLA_HEADS, tq, 2 * KV_LORA), F32),
            ],
        ),
        out_shape=jax.ShapeDtypeStruct((batch * seq, MLA_HEADS * MLA_V), BF16),
        compiler_params=_cparams(("parallel", "arbitrary")),
        name="mla_prompt",
    )(qi_tab, ki_tab, qcat, kcat, jnp.concatenate([cbf, jnp.ones_like(cbf)], axis=1), wuv_bd)


SB_CHUNK = 256


def _suffix_matrix(n):
    j = np.arange(n)[:, None]
    s = np.arange(n)[None, :]
    t = (j > s).astype(np.float32)
    return jnp.asarray(np.concatenate([t, t], axis=0), dtype=BF16)


def _sb_terms(z):
    lp = jnp.log(1.0 + jnp.exp(-jnp.abs(z)))
    log_beta = jnp.minimum(z, 0.0) - lp
    return log_beta, log_beta - z


def _split_bf16(x):
    hi = x.astype(BF16)
    return hi, (x - hi.astype(F32)).astype(BF16)


SB_DEAD_LOG = -104.0


def _sb_chunk(q2, k, v, tt, carry, keep, tq):
    res = []

    def gen():
        z = _mm_nt(q2, k)
        yield
        log_beta, l1 = _sb_terms(z)
        if keep is not None:
            l1 = jnp.where(keep, l1, 0.0)
        hi, lo = _split_bf16(l1)
        res.append(carry + jnp.sum(l1, axis=1, keepdims=True))
        yield
        after = _mm(jnp.concatenate([hi, lo], axis=1), tt)
        yield
        w = jnp.exp(log_beta + after + carry[:, 0:1])
        if keep is not None:
            w = jnp.where(keep, w, 0.0)
        w = w.astype(BF16)
        yield
        o2 = _mm(w, v)
        first_head = lax.broadcasted_iota(jnp.int32, (1, LANES), 1) < SB_HEAD_DIM
        res.append(jnp.where(first_head, o2[0:tq], o2[tq:2 * tq]))

    return gen(), res


def _sb_kernel(q_ref, k_ref, v_ref, tt_ref, o_ref, carry_scr, acc_scr, *, tq, ck):
    qi = pl.program_id(2)
    n_chunks = tq // ck
    tt = tt_ref[...]
    q2 = jnp.concatenate([q_ref[:, g * SB_HEAD_DIM:(g + 1) * SB_HEAD_DIM] for g in range(SB_GROUP)], axis=0)

    base = pl.multiple_of(qi * tq, tq)
    row = lax.broadcasted_iota(jnp.int32, (SB_GROUP * tq, ck), 0) & (tq - 1)
    col0 = lax.broadcasted_iota(jnp.int32, (SB_GROUP * tq, ck), 1)
    carry = jnp.zeros((SB_GROUP * tq, LANES), F32)
    gens, results = [], []
    carry_box = [carry]

    def diag_chunk(c):
        k = k_ref[pl.ds(base + c * ck, ck), :]
        v = v_ref[pl.ds(base + c * ck, ck), :]
        z = _mm_nt(q2, k)
        yield
        keep = (col0 + c * ck) < row
        log_beta, l1 = _sb_terms(z)
        l1 = jnp.where(keep, l1, 0.0)
        hi, lo = _split_bf16(l1)
        carry_in = carry_box[-1]
        carry_box.append(carry_in + jnp.sum(l1, axis=1, keepdims=True))
        yield
        after = _mm(jnp.concatenate([hi, lo], axis=1), tt)
        yield
        w = jnp.where(keep, jnp.exp(log_beta + after + carry_in[:, 0:1]), 0.0).astype(BF16)
        yield
        o2 = _mm(w, v)
        first_head = lax.broadcasted_iota(jnp.int32, (1, LANES), 1) < SB_HEAD_DIM
        results.append(jnp.where(first_head, o2[0:tq], o2[tq:2 * tq]))

    _staggered([diag_chunk(c) for c in range(n_chunks - 1, -1, -1)])
    carry_scr[...] = carry_box[-1]
    acc_scr[...] = functools.reduce(lambda a, b: a + b, results)

    def alive_of(c):
        return (jnp.max(c) >= SB_DEAD_LOG).astype(jnp.int32)

    def cond(st):
        j, alive = st
        return (j >= 0) & (alive > 0)

    def body(st):
        j, _ = st
        start = pl.multiple_of(j * ck, ck)
        gen, res = _sb_chunk(q2, k_ref[pl.ds(start, ck), :], v_ref[pl.ds(start, ck), :], tt,
                             carry_scr[...], None, tq)
        for _ in gen:
            pass
        carry_new, out = res
        carry_scr[...] = carry_new
        acc_scr[...] += out
        return j - 1, alive_of(carry_new)

    lax.while_loop(cond, body, (qi * n_chunks - 1, alive_of(carry_box[-1])))
    o_ref[...] = acc_scr[...].astype(BF16)


def _sb_call(sq, k_heads, v_heads, tt, batch, seq, tq, ck):
    nq = seq // tq
    gw = SB_GROUP * SB_HEAD_DIM
    return pl.pallas_call(
        functools.partial(_sb_kernel, tq=tq, ck=ck),
        grid=(batch, SB_KV_HEADS, nq),
        in_specs=[
            pl.BlockSpec((tq, gw), lambda b, kv, i: (b * nq + i, kv)),
            pl.BlockSpec((None, None, seq, SB_HEAD_DIM), lambda b, kv, i: (b, kv, 0, 0)),
            pl.BlockSpec((None, None, seq, gw), lambda b, kv, i: (b, kv, 0, 0)),
            pl.BlockSpec(tt.shape, lambda b, kv, i: (0, 0)),
        ],
        out_specs=pl.BlockSpec((tq, gw), lambda b, kv, i: (b * nq + i, kv)),
        scratch_shapes=[
            pltpu.VMEM((SB_GROUP * tq, LANES), F32),
            pltpu.VMEM((tq, gw), F32),
        ],
        out_shape=jax.ShapeDtypeStruct((batch * seq, SB_HEADS * SB_HEAD_DIM), BF16),
        compiler_params=_cparams(("parallel", "parallel", "arbitrary")),
        name="sb_prompt",
    )(sq, k_heads, v_heads, tt)


def _absorb_kernel(q_ref, w_ref, o_ref):
    for h in range(MLA_HEADS):
        o_ref[:, h * LANES:(h + 1) * LANES] = _mm(q_ref[:, h * LANES:(h + 1) * LANES], w_ref[h]).astype(BF16)


def _absorb_call(q_s, wabs):
    n = q_s.shape[0]
    return pl.pallas_call(
        _absorb_kernel,
        grid=(1,),
        in_specs=[pl.BlockSpec(q_s.shape, lambda i: (0, 0)), pl.BlockSpec(wabs.shape, lambda i: (0, 0, 0))],
        out_specs=pl.BlockSpec((n, MLA_HEADS * LANES), lambda i: (0, 0)),
        out_shape=jax.ShapeDtypeStruct((n, MLA_HEADS * LANES), BF16),
        compiler_params=_cparams(("arbitrary",)),
        name="absorb_q",
    )(q_s, wabs)


def _sample_sb_tile(qbd, tt, kt_pages, vt_pages, carry_in, keep, n_sb):
    res = []

    def gen():
        n = len(kt_pages)
        z = jnp.concatenate([_mm(qbd, p.astype(BF16)) for p in kt_pages], axis=0)
        yield
        log_beta, l1 = _sb_terms(z)
        if keep is not None:
            l1 = jnp.where(keep, l1, 0.0)
        hi, lo = _split_bf16(l1)
        tot = jnp.sum(l1, axis=1, keepdims=True)
        run = carry_in
        carries = [None] * n
        for i in range(n - 1, -1, -1):
            carries[i] = run
            run = run + tot[i * n_sb:(i + 1) * n_sb]
        res.append(run)
        carry = jnp.concatenate([c[:, 0:1] for c in carries], axis=0)
        yield
        after = _mm(jnp.concatenate([hi, lo], axis=1), tt)
        yield
        w = jnp.exp(log_beta + after + carry)
        if keep is not None:
            w = jnp.where(keep, w, 0.0)
        wl = jnp.concatenate([w[i * n_sb:(i + 1) * n_sb].astype(BF16) for i in range(n)], axis=1)
        vt = jnp.concatenate([p.astype(BF16) for p in vt_pages], axis=1)
        res.append(_mm_nt(vt, wl))

    return gen(), res


def _sample_kernel(*refs, n_pg, n_head, dec_seq, page):
    pt_ref = refs[0]
    del pt_ref
    qabs_ref, qr_ref, qbd_ref, wukt_ref, tt_ref, nlat_ref, nkr_ref, nk_ref, nv_ref = refs[1:10]
    lat_refs = refs[10:10 + n_pg]
    kr_refs = refs[10 + n_pg:10 + 2 * n_pg]
    o = 10 + 2 * n_pg
    k_refs = refs[o:o + n_head]
    v_refs = refs[o + n_head:o + 2 * n_head]
    o += 2 * n_head
    olat_ref, osb_ref, ocarry_ref = refs[o:o + 3]
    wst_scr, m_scr, l_scr, acc_scr = refs[o + 3:]

    j = pl.program_id(1)
    n_rows = MLA_HEADS * dec_seq
    n_sb = SB_HEADS * dec_seq
    kd = SB_KV_HEADS * SB_HEAD_DIM

    def mla_tile(c_pages, krt_pages, mask):
        cb = jnp.concatenate([p.astype(BF16) for p in c_pages], axis=0)
        a = _mm_nt(wst_scr[...], cb)
        n_keys = a.shape[1]
        yield
        krt = jnp.concatenate([p.astype(BF16) for p in krt_pages], axis=1)
        ssq = jnp.concatenate(
            [jnp.sum(jnp.square(a[h * MLA_NOPE:(h + 1) * MLA_NOPE]), axis=0, keepdims=True)
             for h in range(MLA_HEADS)], axis=0)
        r = lax.rsqrt(ssq * (1.0 / MLA_NOPE) + EPS)
        sn = a[MLA_HEADS * MLA_NOPE:]
        s = sn * jnp.concatenate([r] * dec_seq, axis=0) + _mm(qr_ref[...], krt)
        if mask is not None:
            s = jnp.where(mask(n_rows, n_keys, MLA_HEADS, True), s, NEG_INF)
        yield
        m_prev = m_scr[...]
        m_new = jnp.maximum(m_prev, jnp.max(s, axis=1, keepdims=True))
        alpha = jnp.exp(m_prev - m_new)
        p = jnp.exp(s - m_new[:, 0:1])
        l_scr[...] = alpha * l_scr[...] + jnp.sum(p, axis=1, keepdims=True)
        m_scr[...] = m_new
        yield
        acc_scr[...] = alpha * acc_scr[...] + _mm(p.astype(BF16), cb)

    def new_mask(n_r, n_k, div, inclusive):
        row = lax.broadcasted_iota(jnp.int32, (n_r, n_k), 0)
        col = lax.broadcasted_iota(jnp.int32, (n_r, n_k), 1)
        qpos = row // div if inclusive else row % div
        return (col <= qpos) if inclusive else (col < qpos)

    @pl.when(j == 0)
    def _():
        wst_scr[0:MLA_HEADS * MLA_NOPE, :] = wukt_ref[...]
        wst_scr[MLA_HEADS * MLA_NOPE:, :] = qabs_ref[...]
        m_scr[...] = jnp.full(m_scr.shape, NEG_INF, F32)
        l_scr[...] = jnp.zeros(l_scr.shape, F32)
        acc_scr[...] = jnp.zeros(acc_scr.shape, F32)
        tt = tt_ref[...]
        qbd = qbd_ref[...]
        g_new, r_new = _sample_sb_tile(qbd, tt, [nk_ref[...]], [nv_ref[...]],
                                       jnp.zeros((n_sb, LANES), F32), new_mask(n_sb, page, dec_seq, False), n_sb)
        _staggered([mla_tile([nlat_ref[...]], [nkr_ref[...]], new_mask), g_new], offset=0)
        g_head, r_head = _sample_sb_tile(qbd, tt, [r[...].reshape(kd, page) for r in k_refs],
                                         [r[...].reshape(kd, page) for r in v_refs], r_new[0], None, n_sb)
        for _ in g_head:
            pass
        ocarry_ref[...] = r_head[0]
        osb_ref[...] = r_new[1] + r_head[1]

    for _ in mla_tile([r[...] for r in lat_refs], [r[...] for r in kr_refs], None):
        pass

    @pl.when(j == pl.num_programs(1) - 1)
    def _():
        olat_ref[...] = acc_scr[...] / l_scr[...]


def _sample_tail_kernel(*refs, n_pg, dec_seq, page):
    pt_ref, alive_ref = refs[0:2]
    del pt_ref
    qbd_ref, tt_ref, acc_in_ref, carry_in_ref = refs[2:6]
    k_refs = refs[6:6 + n_pg]
    v_refs = refs[6 + n_pg:6 + 2 * n_pg]
    o_ref = refs[6 + 2 * n_pg]
    carry_scr, acc_scr = refs[7 + 2 * n_pg:]
    b = pl.program_id(0)
    j = pl.program_id(1)
    n_sb = SB_HEADS * dec_seq
    kd = SB_KV_HEADS * SB_HEAD_DIM

    @pl.when(j == 0)
    def _():
        carry_scr[...] = carry_in_ref[...]
        acc_scr[...] = acc_in_ref[...]

    @pl.when(alive_ref[b] > 0)
    def _():
        gen, res = _sample_sb_tile(qbd_ref[...], tt_ref[...], [r[...].reshape(kd, page) for r in k_refs],
                                   [r[...].reshape(kd, page) for r in v_refs], carry_scr[...], None, n_sb)
        for _ in gen:
            pass
        carry_scr[...] = res[0]
        acc_scr[...] += res[1]

    @pl.when(j == pl.num_programs(1) - 1)
    def _():
        o_ref[...] = acc_scr[...]


def _sample_call(layer, page_table, qabs, qr, qbd, wukt, tt, nlat, nkr, nk, nv,
                 cache_lat, cache_krt, cache_kt, cache_vt, n_pg, n_head):
    dec_batch, n_pages = page_table.shape
    page = cache_lat.shape[2]
    dec_seq = qabs.shape[1] // MLA_HEADS
    n_steps = n_pages // n_pg
    n_rows = MLA_HEADS * dec_seq
    n_sb = SB_HEADS * dec_seq
    kd = SB_KV_HEADS * SB_HEAD_DIM

    def per_seq(a):
        return pl.BlockSpec((None,) + a.shape[1:], lambda b, j, pt: (b,) + (0,) * (a.ndim - 1))

    def const(a):
        return pl.BlockSpec(a.shape, lambda b, j, pt: (0,) * a.ndim)

    def paged(a, g):
        nd = a.ndim - 2
        return pl.BlockSpec(
            (None, None) + a.shape[2:],
            lambda b, j, pt: (layer, pt[b, (n_steps - 1 - j) * n_pg + g]) + (0,) * nd)

    def head_page(a, g):
        nd = a.ndim - 2
        return pl.BlockSpec(
            (None, None) + a.shape[2:],
            lambda b, j, pt: (layer, pt[b, n_pages - n_head + g]) + (0,) * nd)

    in_specs = [per_seq(qabs), per_seq(qr), per_seq(qbd), const(wukt), const(tt),
                per_seq(nlat), per_seq(nkr), per_seq(nk), per_seq(nv)]
    operands = [qabs, qr, qbd, wukt, tt, nlat, nkr, nk, nv]
    for cache in (cache_lat, cache_krt):
        for g in range(n_pg):
            in_specs.append(paged(cache, g))
            operands.append(cache)
    for cache in (cache_kt, cache_vt):
        for g in range(n_head):
            in_specs.append(head_page(cache, g))
            operands.append(cache)

    return pl.pallas_call(
        functools.partial(_sample_kernel, n_pg=n_pg, n_head=n_head, dec_seq=dec_seq, page=page),
        grid_spec=pltpu.PrefetchScalarGridSpec(
            num_scalar_prefetch=1,
            grid=(dec_batch, n_steps),
            in_specs=in_specs,
            out_specs=(
                pl.BlockSpec((None, n_rows, KV_LORA), lambda b, j, pt: (b, 0, 0)),
                pl.BlockSpec((None, kd, n_sb), lambda b, j, pt: (b, 0, 0)),
                pl.BlockSpec((None, n_sb, LANES), lambda b, j, pt: (b, 0, 0)),
            ),
            scratch_shapes=[
                pltpu.VMEM((MLA_HEADS * MLA_NOPE + n_rows, KV_LORA), BF16),
                pltpu.VMEM((n_rows, LANES), F32),
                pltpu.VMEM((n_rows, LANES), F32),
                pltpu.VMEM((n_rows, KV_LORA), F32),
            ],
        ),
        out_shape=(
            jax.ShapeDtypeStruct((dec_batch, n_rows, KV_LORA), F32),
            jax.ShapeDtypeStruct((dec_batch, kd, n_sb), F32),
            jax.ShapeDtypeStruct((dec_batch, n_sb, LANES), F32),
        ),
        compiler_params=_cparams(("parallel", "arbitrary")),
        name="sample_attn",
    )(page_table, *operands)


def _sample_tail_call(layer, page_table, alive, qbd, tt, acc_in, carry_in, cache_kt, cache_vt, n_tail, n_pg):
    dec_batch = page_table.shape[0]
    page = cache_kt.shape[-1]
    n_sb = qbd.shape[1]
    dec_seq = n_sb // SB_HEADS
    n_steps = n_tail // n_pg
    kd = SB_KV_HEADS * SB_HEAD_DIM

    def per_seq(a):
        return pl.BlockSpec((None,) + a.shape[1:], lambda b, j, pt, al: (b,) + (0,) * (a.ndim - 1))

    def paged(a, g):
        nd = a.ndim - 2
        return pl.BlockSpec(
            (None, None) + a.shape[2:],
            lambda b, j, pt, al: (layer, jnp.where(al[b] > 0, pt[b, (n_steps - 1 - j) * n_pg + g], 0)) + (0,) * nd)

    in_specs = [per_seq(qbd), pl.BlockSpec(tt.shape, lambda b, j, pt, al: (0, 0)), per_seq(acc_in), per_seq(carry_in)]
    operands = [qbd, tt, acc_in, carry_in]
    for cache in (cache_kt, cache_vt):
        for g in range(n_pg):
            in_specs.append(paged(cache, g))
            operands.append(cache)

    return pl.pallas_call(
        functools.partial(_sample_tail_kernel, n_pg=n_pg, dec_seq=dec_seq, page=page),
        grid_spec=pltpu.PrefetchScalarGridSpec(
            num_scalar_prefetch=2,
            grid=(dec_batch, n_steps),
            in_specs=in_specs,
            out_specs=pl.BlockSpec((None, kd, n_sb), lambda b, j, pt, al: (b, 0, 0)),
            scratch_shapes=[pltpu.VMEM((n_sb, LANES), F32), pltpu.VMEM((kd, n_sb), F32)],
        ),
        out_shape=jax.ShapeDtypeStruct((dec_batch, kd, n_sb), F32),
        compiler_params=_cparams(("parallel", "arbitrary")),
        name="sample_sb_tail",
    )(page_table, alive, *operands)


def _uv_kernel(x_ref, w_ref, o_ref):
    o_ref[...] = _mm(x_ref[...].astype(BF16), w_ref[...]).astype(BF16)


def _uv_call(o_lat, wuv_bd):
    n = o_lat.shape[0]
    return pl.pallas_call(
        _uv_kernel,
        grid=(1,),
        in_specs=[pl.BlockSpec(o_lat.shape, lambda i: (0, 0)), pl.BlockSpec(wuv_bd.shape, lambda i: (0, 0))],
        out_specs=pl.BlockSpec((n, MLA_HEADS * MLA_V), lambda i: (0, 0)),
        out_shape=jax.ShapeDtypeStruct((n, MLA_HEADS * MLA_V), BF16),
        compiler_params=_cparams(("arbitrary",)),
        name="sample_uv",
    )(o_lat, wuv_bd)


def _merge_kernel(*refs, n_prompt_tiles, with_router):
    if with_router:
        (oap_ref, obp_ref, oas_ref, obs_ref, g_ref, h_ref, wa_ref, wb_ref, wo_ref, gf_ref, wrh_ref, wrl_ref,
         hn_ref, xn_ref, ids_ref, wts_ref) = refs
    else:
        (oap_ref, obp_ref, oas_ref, obs_ref, g_ref, h_ref, wa_ref, wb_ref, wo_ref, gf_ref,
         hn_ref, xn_ref) = refs
    is_prompt = pl.program_id(0) < n_prompt_tiles
    oa = jnp.where(is_prompt, oap_ref[...], oas_ref[...])
    ob = jnp.where(is_prompt, obp_ref[...], obs_ref[...])
    br_a = _mm(oa, wa_ref[...])
    br_b = _mm(ob, wb_ref[...])
    mixed = g_ref[:, 0:D_MODEL].astype(F32) * br_a + g_ref[:, D_MODEL:2 * D_MODEL].astype(F32) * br_b
    hn = h_ref[...] + _mm(mixed.astype(BF16), wo_ref[...])
    hn_ref[...] = hn
    xn = hn * lax.rsqrt(jnp.mean(hn * hn, axis=-1, keepdims=True) + EPS) * gf_ref[...]
    xn_ref[...] = xn.astype(xn_ref.dtype)
    if with_router:
        xh, xl = _split_bf16(xn)
        logits = _mm(xh, wrh_ref[...]) + (_mm(xl, wrh_ref[...]) + _mm(xh, wrl_ref[...]))
        lane = lax.broadcasted_iota(jnp.int32, logits.shape, 1)
        valid = lane < N_EXPERTS
        lg = jnp.where(valid, logits, NEG_INF)
        m1 = jnp.max(lg, axis=1, keepdims=True)
        i1 = jnp.min(jnp.where(lg == m1, lane, LANES), axis=1, keepdims=True)
        lg2 = jnp.where(lane == i1, NEG_INF, lg)
        m2 = jnp.max(lg2, axis=1, keepdims=True)
        i2 = jnp.min(jnp.where(lg2 == m2, lane, LANES), axis=1, keepdims=True)
        e2 = jnp.exp(m2 - m1)
        w1 = 1.0 / (1.0 + e2)
        w2 = e2 / (1.0 + e2)
        ids_ref[...] = jnp.where(lane == 0, i1, jnp.where(lane == 1, i2, 0))
        wts_ref[...] = jnp.where(lane == 0, w1, jnp.where(lane == 1, w2, 0.0))


def _merge_call(oa_p, ob_p, oa_s, ob_s, gates, h, wa, wb, wo, gf, wr, tm):
    T = h.shape[0]
    n_p = oa_p.shape[0] // tm
    with_router = wr is not None
    row = lambda w: pl.BlockSpec((tm, w), lambda i: (i, 0))
    full = lambda a: pl.BlockSpec(a.shape, lambda i: (0, 0))
    p_spec = pl.BlockSpec((tm, oa_p.shape[1]), lambda i: (jnp.minimum(i, n_p - 1), 0))
    s_spec = pl.BlockSpec((tm, oa_s.shape[1]), lambda i: (jnp.maximum(i - n_p, 0), 0))
    in_specs = [p_spec, p_spec, s_spec, s_spec, row(2 * D_MODEL), row(D_MODEL), full(wa), full(wb), full(wo), full(gf)]
    operands = [oa_p, ob_p, oa_s, ob_s, gates, h, wa, wb, wo, gf]
    out_shape = [jax.ShapeDtypeStruct((T, D_MODEL), F32),
                 jax.ShapeDtypeStruct((T, D_MODEL), F32 if with_router else BF16)]
    out_specs = [row(D_MODEL), row(D_MODEL)]
    if with_router:
        wr_hi, wr_lo = _split_bf16(wr)
        in_specs += [full(wr_hi), full(wr_lo)]
        operands += [wr_hi, wr_lo]
        out_shape += [jax.ShapeDtypeStruct((T, LANES), jnp.int32), jax.ShapeDtypeStruct((T, LANES), F32)]
        out_specs += [row(LANES), row(LANES)]
    return pl.pallas_call(
        functools.partial(_merge_kernel, n_prompt_tiles=n_p, with_router=with_router),
        grid=(T // tm,),
        in_specs=in_specs,
        out_specs=tuple(out_specs),
        out_shape=tuple(out_shape),
        compiler_params=_cparams(("parallel",)),
        name="merge",
    )(*operands)


def _swiglu_tile(x, wg, wu):
    a = _mm(x, wg)
    return (a * (1.0 / (1.0 + jnp.exp(-a))) * _mm(x, wu)).astype(BF16)


def _ffn_kernel(x_ref, h_ref, wg_ref, wu_ref, wd_ref, o_ref, acc_scr):
    f = pl.program_id(1)

    @pl.when(f == 0)
    def _():
        acc_scr[...] = h_ref[...]

    acc_scr[...] += _mm(_swiglu_tile(x_ref[...], wg_ref[...], wu_ref[...]), wd_ref[...])

    @pl.when(f == pl.num_programs(1) - 1)
    def _():
        o_ref[...] = acc_scr[...]


def _ffn_call(xn, h, wg, wu, wd, tm, tf):
    T = h.shape[0]
    d_ff = wg.shape[1]
    return pl.pallas_call(
        _ffn_kernel,
        grid=(T // tm, d_ff // tf),
        in_specs=[
            pl.BlockSpec((tm, D_MODEL), lambda i, f: (i, 0)),
            pl.BlockSpec((tm, D_MODEL), lambda i, f: (i, 0)),
            pl.BlockSpec((D_MODEL, tf), lambda i, f: (0, f)),
            pl.BlockSpec((D_MODEL, tf), lambda i, f: (0, f)),
            pl.BlockSpec((tf, D_MODEL), lambda i, f: (f, 0)),
        ],
        out_specs=pl.BlockSpec((tm, D_MODEL), lambda i, f: (i, 0)),
        out_shape=jax.ShapeDtypeStruct((T, D_MODEL), F32),
        scratch_shapes=[pltpu.VMEM((tm, D_MODEL), F32)],
        compiler_params=_cparams(("parallel", "arbitrary")),
        name="ffn_dense",
    )(xn, h, wg, wu, wd)


MOE_TILE = 512


def _row_copy(src_hbm, row, dst, r, sem):
    return pltpu.make_async_copy(src_hbm.at[pl.ds(row, 1)], dst.at[pl.ds(r, 1)], sem)


def _gather_kernel(idx_ref, src_hbm, o_ref, sem, *, rows):
    def start(r, c):
        _row_copy(src_hbm, idx_ref[0, r], o_ref, r, sem).start()
        return c

    lax.fori_loop(0, rows, start, 0, unroll=8)

    def wait(r, c):
        _row_copy(src_hbm, 0, o_ref, r, sem).wait()
        return c

    lax.fori_loop(0, rows, wait, 0, unroll=8)


def _gather_call(src, idx, rows):
    n_tiles = idx.shape[0]
    d = src.shape[1]
    return pl.pallas_call(
        functools.partial(_gather_kernel, rows=rows),
        grid=(n_tiles,),
        in_specs=[
            pl.BlockSpec((None, 1, rows), lambda i: (i, 0, 0), memory_space=pltpu.SMEM),
            pl.BlockSpec(memory_space=pl.ANY),
        ],
        out_specs=pl.BlockSpec((rows, d), lambda i: (i, 0)),
        out_shape=jax.ShapeDtypeStruct((n_tiles * rows, d), src.dtype),
        scratch_shapes=[pltpu.SemaphoreType.DMA(())],
        compiler_params=_cparams(("arbitrary",)),
        name="moe_gather",
    )(idx, src)


def _gmm_kernel(te_ref, tv_ref, x_ref, wg_ref, wu_ref, wd_ref, o_ref, xb_scr, acc_scr):
    del te_ref
    i = pl.program_id(0)
    f = pl.program_id(1)
    used = tv_ref[i] > 0

    @pl.when(used & (f == 0))
    def _():
        xb_scr[...] = x_ref[...].astype(BF16)

    @pl.when(used)
    def _():
        y = _mm(_swiglu_tile(xb_scr[...], wg_ref[...], wu_ref[...]), wd_ref[...])

        @pl.when(f == 0)
        def _():
            acc_scr[...] = y

        @pl.when(f > 0)
        def _():
            acc_scr[...] += y

    @pl.when(f == pl.num_programs(1) - 1)
    def _():
        @pl.when(used)
        def _():
            o_ref[...] = acc_scr[...]

        @pl.when(jnp.logical_not(used))
        def _():
            o_ref[...] = jnp.zeros(o_ref.shape, F32)


def _gmm_call(tile_expert, tile_used, x_sorted, wg, wu, wd, layer, rows, tf):
    n_tiles = x_sorted.shape[0] // rows
    d_ff = wg.shape[3]
    n_f = d_ff // tf
    w_in = lambda i, f, te, tv: (layer, te[i], 0, jnp.where(tv[i] > 0, f, n_f - 1))
    w_out = lambda i, f, te, tv: (layer, te[i], jnp.where(tv[i] > 0, f, n_f - 1), 0)
    return pl.pallas_call(
        _gmm_kernel,
        grid_spec=pltpu.PrefetchScalarGridSpec(
            num_scalar_prefetch=2,
            grid=(n_tiles, n_f),
            in_specs=[
                pl.BlockSpec((rows, D_MODEL), lambda i, f, te, tv: (i, 0)),
                pl.BlockSpec((None, None, D_MODEL, tf), w_in),
                pl.BlockSpec((None, None, D_MODEL, tf), w_in),
                pl.BlockSpec((None, None, tf, D_MODEL), w_out),
            ],
            out_specs=pl.BlockSpec((rows, D_MODEL), lambda i, f, te, tv: (i, 0)),
            scratch_shapes=[pltpu.VMEM((rows, D_MODEL), BF16), pltpu.VMEM((rows, D_MODEL), F32)],
        ),
        out_shape=jax.ShapeDtypeStruct(x_sorted.shape, F32),
        compiler_params=_cparams(("parallel", "arbitrary")),
        name="ffn_experts",
    )(tile_expert, tile_used, x_sorted, wg, wu, wd)


def _combine_kernel(pos_ref, h_ref, w_ref, y_hbm, o_ref, buf, sem, *, rows):
    def start(r, c):
        _row_copy(y_hbm, pos_ref[0, r], buf, r, sem).start()
        return c

    lax.fori_loop(0, 2 * rows, start, 0, unroll=8)

    def wait(r, c):
        _row_copy(y_hbm, 0, buf, r, sem).wait()
        return c

    lax.fori_loop(0, 2 * rows, wait, 0, unroll=8)
    w = w_ref[...]
    o_ref[...] = h_ref[...] + (w[:, 0:1] * buf[0:rows, :] + w[:, 1:2] * buf[rows:2 * rows, :])


def _combine_call(pos, h, wts, y_sorted, rows):
    T = h.shape[0]
    return pl.pallas_call(
        functools.partial(_combine_kernel, rows=rows),
        grid=(T // rows,),
        in_specs=[
            pl.BlockSpec((None, 1, 2 * rows), lambda i: (i, 0, 0), memory_space=pltpu.SMEM),
            pl.BlockSpec((rows, D_MODEL), lambda i: (i, 0)),
            pl.BlockSpec((rows, LANES), lambda i: (i, 0)),
            pl.BlockSpec(memory_space=pl.ANY),
        ],
        out_specs=pl.BlockSpec((rows, D_MODEL), lambda i: (i, 0)),
        out_shape=jax.ShapeDtypeStruct((T, D_MODEL), F32),
        scratch_shapes=[pltpu.VMEM((2 * rows, D_MODEL), F32), pltpu.SemaphoreType.DMA(())],
        compiler_params=_cparams(("arbitrary",)),
        name="moe_combine",
    )(pos, h, wts, y_sorted)


def _route_plan(ids, rows):
    T = ids.shape[0]
    n_slots = 2 * T
    n_tiles = (n_slots + N_EXPERTS * (rows - 1) + rows - 1) // rows
    e_slot = ids.reshape(n_slots)
    onehot = (e_slot[:, None] == jnp.arange(N_EXPERTS, dtype=jnp.int32)[None, :]).astype(jnp.int32)
    rank = jnp.sum((jnp.cumsum(onehot, axis=0) - onehot) * onehot, axis=1)
    counts = jnp.sum(onehot, axis=0)
    padded = ((counts + rows - 1) // rows) * rows
    ends = jnp.cumsum(padded)
    starts = ends - padded
    dest = starts[e_slot] + rank
    src = jnp.zeros((n_tiles * rows,), jnp.int32).at[dest].set(jnp.arange(n_slots, dtype=jnp.int32) // 2)
    tile_start = jnp.arange(n_tiles, dtype=jnp.int32) * rows
    tile_expert = jnp.minimum(jnp.sum((tile_start[:, None] >= ends[None, :]).astype(jnp.int32), axis=1),
                              N_EXPERTS - 1)
    tile_used = (tile_start < ends[-1]).astype(jnp.int32)
    return src.reshape(n_tiles, 1, rows), tile_expert, tile_used, dest.reshape(T, 2)


def _moe_call(xn32, h, ids, wts, wg, wu, wd, layer, tm, tf):
    T = h.shape[0]
    src, tile_expert, tile_used, dest = _route_plan(ids[:, 0:2], MOE_TILE)
    x_sorted = _gather_call(xn32, src, MOE_TILE)
    y_sorted = _gmm_call(tile_expert, tile_used, x_sorted, wg, wu, wd, layer, MOE_TILE, tf)
    pos = dest.reshape(T // tm, tm, 2).transpose(0, 2, 1).reshape(T // tm, 1, 2 * tm)
    return _combine_call(pos, h, wts, y_sorted, tm)


def _head_block_cols(w_rope, w_nope):
    pad = jnp.zeros(w_rope.shape[:-1] + (LANES - MLA_ROPE - MLA_NOPE,), w_rope.dtype)
    blk = jnp.concatenate([w_rope, w_nope, pad], axis=-1)
    return blk.reshape(blk.shape[:-2] + (blk.shape[-2] * LANES,))


def _frame(rope=None, nope=None):
    v = jnp.zeros((LANES,), F32)
    if rope is not None:
        v = v.at[0:MLA_ROPE].set(rope)
    if nope is not None:
        v = v.at[NOPE_LO:NOPE_HI].set(nope)
    return v


def _row(v):
    return jnp.zeros((D_MODEL,), F32).at[0:v.shape[0]].set(v)


def _rope_table(pos):
    inv_freq = ROPE_THETA ** (-jnp.arange(HALF_ROPE, dtype=F32) / HALF_ROPE)
    ang = pos.astype(F32)[:, None] * inv_freq[None, :]
    cos, sin = jnp.cos(ang), jnp.sin(ang)
    n = pos.shape[0]
    z16 = jnp.zeros((n, HALF_ROPE), F32)
    cos_t = jnp.concatenate([cos, cos, jnp.ones((n, LANES - MLA_ROPE), F32)], axis=1)
    sina = jnp.concatenate([-sin, jnp.zeros((n, LANES - HALF_ROPE), F32)], axis=1)
    sinb = jnp.concatenate([z16, sin, jnp.zeros((n, LANES - MLA_ROPE), F32)], axis=1)
    return jnp.concatenate([cos_t, sina, sinb], axis=1)


def kernel(x_prompt, x_sample, cache_mla_latent, cache_mla_krope, cache_sb_k, cache_sb_v, page_table, g_attn_norm, w_in, g_q_lat, w_q_up, g_q_nope, g_q_rope, g_kv_lat, g_k_rope, w_uk, g_k_nope, w_uv, w_mla_out, w_sb_out, w_o, g_ffn_norm, w_ffn_gate, w_ffn_up, w_ffn_down, w_router, w_exp_gate, w_exp_up, w_exp_down):
    batch, seq, _ = x_prompt.shape
    dec_batch, dec_seq, _ = x_sample.shape
    depth = w_in.shape[0]
    n_pages = page_table.shape[1]
    page = cache_mla_latent.shape[2]
    past_len = n_pages * page
    t_p = batch * seq
    t_s = dec_batch * dec_seq
    T = t_p + t_s
    assert page == LANES and dec_seq <= page

    tm = _pick(math.gcd(t_p, t_s), (512, 256, 128, 64, 32, 16))
    tq = _pick(seq, (512, 256, 128))
    ck = min(SB_CHUNK, tq)
    n_pg = _pick(n_pages, (16, 8, 4, 2, 1))
    n_head = min(2, n_pages)
    n_tail = n_pages - n_head
    n_pg_tail = _pick(n_tail, (31, 16, 15, 12, 10, 8, 6, 5, 4, 3, 2, 1)) if n_tail else 0
    tm_ffn = _pick(T, (768, 512, 384, 256, 192, 128, 64, 32, 16))

    pos = jnp.concatenate([
        jnp.tile(jnp.arange(seq, dtype=jnp.int32), batch),
        jnp.tile(past_len + jnp.arange(dec_seq, dtype=jnp.int32), dec_batch)])
    tab = _rope_table(pos)
    tt_prompt = _suffix_matrix(ck)
    tt_page = _suffix_matrix(page)

    cache_krt = jnp.transpose(cache_mla_krope, (0, 1, 3, 2))
    cache_kt = jnp.transpose(cache_sb_k, (0, 1, 3, 4, 2))
    cache_vt = jnp.transpose(cache_sb_v, (0, 1, 3, 4, 2))

    w_exp_gate_bf = w_exp_gate.astype(BF16)
    w_exp_up_bf = w_exp_up.astype(BF16)
    w_exp_down_bf = w_exp_down.astype(BF16)

    h = jnp.concatenate([x_prompt.reshape(t_p, D_MODEL), x_sample.reshape(t_s, D_MODEL)], axis=0)
    outs = [[] for _ in range(8)]
    sizes = np.cumsum([Q_LORA, KV_LORA, MLA_ROPE, SB_HEADS * SB_HEAD_DIM, SB_KV_HEADS * SB_HEAD_DIM,
                       SB_KV_HEADS * SB_HEAD_DIM, D_MODEL])

    for l in range(depth):
        w_ql, w_c, w_kr, w_sq, w_sk, w_sv, w_ga, w_gb = jnp.split(w_in[l], sizes, axis=1)
        w_kr_pad = jnp.concatenate([w_kr, jnp.zeros((D_MODEL, LANES - MLA_ROPE), F32)], axis=1)
        w_all = jnp.concatenate([w_ql, w_c, w_kr_pad, w_sq * SB_SCALE, w_sk, w_sv, w_ga, w_gb], axis=1).astype(BF16)
        wq3 = w_q_up[l].reshape(Q_LORA, MLA_HEADS, MLA_NOPE + MLA_ROPE)
        wqu = _head_block_cols(wq3[..., MLA_NOPE:], wq3[..., :MLA_NOPE]).astype(BF16)
        wuk = _head_block_cols(jnp.zeros((KV_LORA, MLA_HEADS, MLA_ROPE), F32), w_uk[l]).astype(BF16)
        gv = jnp.stack([
            g_attn_norm[l], _row(g_q_lat[l]),
            _row(_frame(rope=g_q_rope[l] * MLA_SCALE)), _row(_frame(nope=g_q_nope[l] * MLA_SCALE)),
            _row(g_kv_lat[l]), _row(_frame(rope=g_k_rope[l])), _row(_frame(nope=g_k_nope[l])),
            jnp.zeros((D_MODEL,), F32)])
        wuv_bd = jnp.zeros((MLA_HEADS * KV_LORA, MLA_HEADS * MLA_V), F32)
        for hh in range(MLA_HEADS):
            wuv_bd = wuv_bd.at[hh * KV_LORA:(hh + 1) * KV_LORA, hh * MLA_V:(hh + 1) * MLA_V].set(w_uv[l][:, hh, :])
        wuv_bd = wuv_bd.astype(BF16)

        (qcat, kcat, c, cbf, kr, sq, sk, sv, skbf, svbf, gates) = _proj_call(h, tab, gv, w_all, wqu, wuk, tm)

        oa_p = _mla_call(qcat, kcat, cbf, wuv_bd, batch, seq, tq)
        k_heads = skbf[:t_p].reshape(batch, seq, SB_KV_HEADS, SB_HEAD_DIM).transpose(0, 2, 1, 3)
        v_heads = svbf[:t_p].reshape(batch, seq, SB_KV_HEADS, SB_HEAD_DIM).transpose(0, 2, 1, 3)
        v_heads = jnp.concatenate([v_heads, v_heads], axis=-1)
        ob_p = _sb_call(sq, k_heads, v_heads, tt_prompt, batch, seq, tq, ck)

        q_s = qcat[t_p:]
        wabs = jnp.zeros((MLA_HEADS, LANES, KV_LORA), F32)
        wabs = wabs.at[:, NOPE_LO:NOPE_HI, :].set(
            jnp.transpose(w_uk[l] * g_k_nope[l][None, None, :], (1, 2, 0)))
        qabs = _absorb_call(q_s, wabs.astype(BF16))
        qabs = qabs.reshape(dec_batch, dec_seq * MLA_HEADS, KV_LORA)
        qr = q_s.reshape(dec_batch, dec_seq * MLA_HEADS, LANES)[:, :, 0:MLA_ROPE]
        sq_s = sq[t_p:].reshape(dec_batch, dec_seq, SB_HEADS, SB_HEAD_DIM).transpose(0, 2, 1, 3)
        sq_s = sq_s.reshape(dec_batch, SB_KV_HEADS, SB_GROUP * dec_seq, SB_HEAD_DIM)
        eye = jnp.eye(SB_KV_HEADS, dtype=BF16)
        qbd = (sq_s[:, :, :, None, :] * eye[None, :, None, :, None]).reshape(
            dec_batch, SB_HEADS * dec_seq, SB_KV_HEADS * SB_HEAD_DIM)
        wukt = jnp.transpose(w_uk[l], (1, 2, 0)).reshape(MLA_HEADS * MLA_NOPE, KV_LORA).astype(BF16)
        pad_k = page - dec_seq
        nlat = jnp.pad(c[t_p:].reshape(dec_batch, dec_seq, KV_LORA), ((0, 0), (0, pad_k), (0, 0)))
        nkr = jnp.pad(kr[t_p:].reshape(dec_batch, dec_seq, MLA_ROPE).transpose(0, 2, 1), ((0, 0), (0, 0), (0, pad_k)))
        nk = jnp.pad(sk[t_p:].reshape(dec_batch, dec_seq, -1).transpose(0, 2, 1), ((0, 0), (0, 0), (0, pad_k)))
        nv = jnp.pad(sv[t_p:].reshape(dec_batch, dec_seq, -1).transpose(0, 2, 1), ((0, 0), (0, 0), (0, pad_k)))
        o_lat_s, o_sbt, sb_carry = _sample_call(l, page_table, qabs, qr, qbd, wukt, tt_page, nlat, nkr, nk, nv,
                                                cache_mla_latent, cache_krt, cache_kt, cache_vt, n_pg, n_head)
        if n_tail > 0:
            alive = (jnp.max(sb_carry, axis=(1, 2)) >= SB_DEAD_LOG).astype(jnp.int32)
            o_sbt = lax.cond(
                jnp.max(alive) > 0,
                functools.partial(_sample_tail_call, l, page_table, alive, qbd, tt_page, o_sbt, sb_carry,
                                  cache_kt, cache_vt, n_tail, n_pg_tail),
                lambda: o_sbt)
        o_lat_s = o_lat_s.reshape(dec_batch, dec_seq, MLA_HEADS, KV_LORA).reshape(t_s, MLA_HEADS * KV_LORA)
        oa_s = _uv_call(o_lat_s, wuv_bd)
        o5 = o_sbt.reshape(dec_batch, SB_KV_HEADS, SB_HEAD_DIM, SB_KV_HEADS, SB_GROUP, dec_seq)
        o5 = jnp.stack([o5[:, kv, :, kv] for kv in range(SB_KV_HEADS)], axis=1)
        ob_s = o5.transpose(0, 4, 1, 3, 2).reshape(t_s, SB_HEADS * SB_HEAD_DIM).astype(BF16)

        j = l // 2
        moe = l % 2 == 1
        wr = None
        if moe:
            wr = jnp.concatenate([w_router[j], jnp.zeros((D_MODEL, LANES - N_EXPERTS), F32)], axis=1)
        res = _merge_call(oa_p, ob_p, oa_s, ob_s, gates, h,
                          w_mla_out[l].astype(BF16), w_sb_out[l].astype(BF16), w_o[l].astype(BF16),
                          g_ffn_norm[l][None, :], wr, tm)
        if moe:
            hn, xn32, route_ids, route_wts = res
            h = _moe_call(xn32, hn, route_ids, route_wts, w_exp_gate_bf, w_exp_up_bf, w_exp_down_bf, j,
                          _pick(T, (256, 128, 64, 32, 16, 8)), 512)
        else:
            hn, xn = res
            d_ff = w_ffn_gate.shape[2]
            h = _ffn_call(xn, hn, w_ffn_gate[j].astype(BF16), w_ffn_up[j].astype(BF16),
                          w_ffn_down[j].astype(BF16), tm_ffn, _pick(d_ff, (1408, 512, 256, 128)))

        for idx, a in enumerate((c, kr, sk, sv)):
            outs[idx].append(a[:t_p])
            outs[4 + idx].append(a[t_p:])

    def stack_p(xs, tail):
        return jnp.stack(xs).reshape((depth, batch, seq) + tail)

    def stack_s(xs, tail):
        return jnp.stack(xs).reshape((depth, dec_batch, dec_seq) + tail)

    kv_tail = (SB_KV_HEADS, SB_HEAD_DIM)
    return (h[:t_p].reshape(batch, seq, D_MODEL), h[t_p:].reshape(dec_batch, dec_seq, D_MODEL),
            stack_p(outs[0], (KV_LORA,)), stack_p(outs[1], (MLA_ROPE,)),
            stack_p(outs[2], kv_tail), stack_p(outs[3], kv_tail),
            stack_s(outs[4], (KV_LORA,)), stack_s(outs[5], (MLA_ROPE,)),
            stack_s(outs[6], kv_tail), stack_s(outs[7], kv_tail))
```
